```python
import jax, jax.numpy as jnp
from jax import lax
import numpy as np

D_MODEL = 1024
BATCH = 32
SEQ = 2048
DEPTH = 1
DEC_BATCH = 128
DEC_SEQ = 4
PAST_LEN = 8192
PAGE_SIZE = 128

HG_HEADS = 4
HG_DK = 128
HG_DV = 128
HG_WIDTH = HG_HEADS * HG_DV
HG_CHUNK = 32
NSA_HEADS = 8
NSA_KV_HEADS = 2
NSA_HPG = NSA_HEADS // NSA_KV_HEADS
NSA_HD = 64
NSA_WIDTH = NSA_HEADS * NSA_HD
KV_WIDTH = NSA_KV_HEADS * NSA_HD
CMP_BLOCK = 32
CMP_STRIDE = 16
CMP_RATIO = CMP_BLOCK // CMP_STRIDE
CMP_HIDDEN = 128
SEL_BLOCK = 64
N_SEL = 16
WINDOW = 512
WIN_Q_BLOCK = 128
SEL_Q_BLOCK = 16
MIX_WIDTH = HG_WIDTH + NSA_WIDTH
N_MEM = 256
MEM_HEADS = 4
MEM_HD = D_MODEL // MEM_HEADS
PEER_HEADS = 8
PEER_NKEYS = 128
PEER_EXPERTS = PEER_NKEYS * PEER_NKEYS
PEER_KEY_DIM = 128
PEER_TOPK = 16
PEER_TOKEN_BLOCK = 256
DN_ALPHA = (2.0 * DEPTH) ** 0.25
DN_BETA = (8.0 * DEPTH) ** -0.25
LN_EPS = 1e-5
NEG_INF = -1e30

IN_SIZES = (HG_HEADS * HG_DK, HG_HEADS * HG_DK, HG_WIDTH, HG_WIDTH,
            NSA_WIDTH, KV_WIDTH, KV_WIDTH, KV_WIDTH, KV_WIDTH, KV_WIDTH, KV_WIDTH, 3 * NSA_HEADS)
IN_IS_VALUE = (False, False, True, False, False, False, True, False, True, False, True, False)
IN_OFFSETS = tuple(sum(IN_SIZES[:i + 1]) for i in range(len(IN_SIZES) - 1))
IN_COLS = sum(IN_SIZES)

kernel_name = 'hymba_hgrn2_nsa_peer_decode_step'


def layer_norm(x, g, b):
    xf = x.astype(jnp.float32)
    xc = xf - jnp.mean(xf, -1, keepdims=True)
    var = jnp.mean(xc * xc, -1, keepdims=True)
    return (xc * lax.rsqrt(var + LN_EPS) * g + b).astype(x.dtype)


def rms_norm(x, g):
    xf = x.astype(jnp.float32)
    return xf * lax.rsqrt(jnp.mean(xf * xf, -1, keepdims=True) + LN_EPS) * g


def alibi_slopes():
    s = 2.0 ** (-8.0 * np.arange(1, NSA_HEADS + 1) / NSA_HEADS)
    return jnp.asarray(s, jnp.float32).reshape(NSA_KV_HEADS, NSA_HPG)


def masked_softmax(s, mask):
    p = jax.nn.softmax(jnp.where(mask, s, NEG_INF), axis=-1)
    return jnp.where(mask, p, 0.0)


def hgrn_features(zq, zf, zi, lb):
    B, T, _ = zq.shape
    zf = zf.astype(jnp.float32)
    f = lb + (1.0 - lb) * jax.nn.sigmoid(zf)
    log_f = jnp.log(f)
    k = (1.0 - lb) * jax.nn.sigmoid(-zf)
    heads = lambda a, d: a.astype(jnp.float32).reshape(B, T, HG_HEADS, d).transpose(0, 2, 1, 3)
    return heads(zq, HG_DK), heads(k, HG_DK), heads(zi, HG_DV), heads(log_f, HG_DK)


def hgrn_recurrence(q, k, v, log_f, s0, chunk):
    B, H, T, _ = q.shape
    n = T // chunk
    split = lambda a: jnp.moveaxis(a.reshape(B, H, n, chunk, a.shape[-1]), 2, 0)
    causal = jnp.tril(jnp.ones((chunk, chunk), bool))[:, :, None]

    def step(S, inp):
        qc, kc, vc, lc = inp
        G = jnp.cumsum(lc, axis=2)
        rel = jnp.where(causal, G[:, :, :, None, :] - G[:, :, None, :, :], -jnp.inf)
        att = jnp.einsum('bhtd,bhsd,bhtsd->bhts', qc, kc, jnp.exp(rel))
        o = jnp.einsum('bhtd,bhde->bhte', qc * jnp.exp(G), S) + jnp.einsum('bhts,bhse->bhte', att, vc)
        G_end = G[:, :, -1]
        S = jnp.exp(G_end)[..., None] * S + jnp.einsum('bhsd,bhse->bhde', kc * jnp.exp(G_end[:, :, None] - G), vc)
        return S, o

    S, o = lax.scan(step, s0, (split(q), split(k), split(v), split(log_f)))
    return jnp.moveaxis(o, 0, 2).reshape(B, H, T, -1), S


def hgrn_readout(o, zg, norm_g):
    B, H, T, _ = o.shape
    o = rms_norm(o, norm_g).transpose(0, 2, 1, 3).reshape(B, T, HG_WIDTH)
    return o * jax.nn.sigmoid(zg.astype(jnp.float32))


def cmp_partials(k, w1):
    B, T, G, D = k.shape
    n_sub = T // CMP_STRIDE
    ks = k[:, :n_sub * CMP_STRIDE].reshape(B, n_sub, CMP_STRIDE, G, D)
    return jnp.einsum('bnpgd,rpdh->rbngh', ks, w1.reshape(CMP_RATIO, CMP_STRIDE, D, CMP_HIDDEN))


def cmp_finish(a, b1, w2, b2):
    n_cmp = a.shape[2] - CMP_RATIO + 1
    h = b1 + sum(a[r, :, r:r + n_cmp] for r in range(CMP_RATIO))
    return jnp.einsum('bngh,hd->bngd', jax.nn.gelu(h), w2) + b2


def nsa_cmp_attend(q, qpos, kc, vc, slopes):
    B, Tq = q.shape[:2]
    n_cmp = kc.shape[1]
    qg = q.reshape(B, Tq, NSA_KV_HEADS, NSA_HPG, NSA_HD)
    s = jnp.einsum('btgjd,bcgd->bgjtc', qg, kc).astype(jnp.float32) * NSA_HD ** -0.5
    kend = jnp.arange(n_cmp) * CMP_STRIDE + CMP_BLOCK - 1
    dist = qpos[:, None] - kend[None, :]
    p = masked_softmax(s - slopes[:, :, None, None] * dist.astype(jnp.float32), dist >= 0)
    o = jnp.einsum('bgjtc,bcgd->btgjd', p.astype(vc.dtype), vc).reshape(B, Tq, NSA_HEADS, NSA_HD)
    return o, p


def cmp_to_slc(n_cmp, n_slc):
    cs = np.arange(n_cmp)[:, None] * CMP_STRIDE
    ss = np.arange(n_slc)[None, :] * SEL_BLOCK
    shared = np.clip(np.minimum(cs + CMP_BLOCK, ss + SEL_BLOCK) - np.maximum(cs, ss), 0, None)
    return jnp.asarray(shared / CMP_STRIDE, jnp.float32)


def nsa_select(p_cmp, qpos, n_slc):
    n_cmp = p_cmp.shape[-1]
    imp = jnp.einsum('bgjtc,cn->bgtn', p_cmp, cmp_to_slc(n_cmp, n_slc))
    blk = jnp.arange(n_slc)
    elig = (blk * SEL_BLOCK)[None, :] <= qpos[:, None]
    force = (blk[None, :] == (qpos // SEL_BLOCK)[:, None]) | (blk == 0)[None, :]
    imp = jnp.where(force, jnp.inf, jnp.where(elig, imp, -jnp.inf))
    return lax.top_k(imp, min(N_SEL, n_slc))[1]


def nsa_slc_attend(q, qpos, kpos, kg, vg, slopes):
    B, Tq = q.shape[:2]
    qg = q.reshape(B, Tq, NSA_KV_HEADS, NSA_HPG, NSA_HD)
    s = jnp.einsum('btgjd,bgtnsd->bgjtns', qg, kg).astype(jnp.float32) * NSA_HD ** -0.5
    dist = qpos[None, None, :, None, None] - kpos
    s = s - slopes[None, :, :, None, None, None] * dist[:, :, None].astype(jnp.float32)
    shp = s.shape
    mask = (dist >= 0).reshape(B, NSA_KV_HEADS, 1, Tq, -1)
    p = masked_softmax(s.reshape(*shp[:4], -1), mask).reshape(shp)
    o = jnp.einsum('bgjtns,bgtnsd->btgjd', p.astype(vg.dtype), vg)
    return o.reshape(B, Tq, NSA_HEADS, NSA_HD)


def nsa_win_attend(q, qpos, k, v, kpos, slopes):
    B, Tq = q.shape[:2]
    qg = q.reshape(B, Tq, NSA_KV_HEADS, NSA_HPG, NSA_HD)
    s = jnp.einsum('btgjd,bsgd->bgjts', qg, k).astype(jnp.float32) * NSA_HD ** -0.5
    dist = qpos[:, None] - kpos[None, :]
    mask = (dist >= 0) & (dist < WINDOW) & (kpos >= 0)[None, :]
    p = masked_softmax(s - slopes[:, :, None, None] * dist.astype(jnp.float32), mask)
    o = jnp.einsum('bgjts,bsgd->btgjd', p.astype(v.dtype), v)
    return o.reshape(B, Tq, NSA_HEADS, NSA_HD)


def prompt_slc(q, pos, idx, k, v, slopes):
    B, T = q.shape[:2]
    nb = T // SEL_Q_BLOCK
    qb = q.reshape(B, nb, SEL_Q_BLOCK, NSA_HEADS, NSA_HD).swapaxes(0, 1)
    ib = idx.reshape(B, NSA_KV_HEADS, nb, SEL_Q_BLOCK, -1).transpose(2, 0, 1, 3, 4)
    pb = pos.reshape(nb, SEL_Q_BLOCK)
    bi = jnp.arange(B)[:, None, None, None, None]
    gi = jnp.arange(NSA_KV_HEADS)[None, :, None, None, None]

    def one(args):
        qq, ii, pp = args
        kpos = ii[..., None] * SEL_BLOCK + jnp.arange(SEL_BLOCK)
        return nsa_slc_attend(qq, pp, kpos, k[bi, kpos, gi], v[bi, kpos, gi], slopes)

    o = lax.map(one, (qb, ib, pb))
    return o.swapaxes(0, 1).reshape(B, T, NSA_HEADS, NSA_HD)


def prompt_win(q, k, v, slopes):
    B, T = q.shape[:2]
    nb = T // WIN_Q_BLOCK
    span = WINDOW + WIN_Q_BLOCK
    pad = ((0, 0), (WINDOW, 0), (0, 0), (0, 0))
    kp, vp = jnp.pad(k, pad), jnp.pad(v, pad)
    qb = q.reshape(B, nb, WIN_Q_BLOCK, NSA_HEADS, NSA_HD).swapaxes(0, 1)

    def one(args):
        i, qq = args
        start = i * WIN_Q_BLOCK
        kk = lax.dynamic_slice_in_dim(kp, start, span, axis=1)
        vv = lax.dynamic_slice_in_dim(vp, start, span, axis=1)
        qpos = start + jnp.arange(WIN_Q_BLOCK)
        kpos = start - WINDOW + jnp.arange(span)
        return nsa_win_attend(qq, qpos, kk, vv, kpos, slopes)

    o = lax.map(one, (jnp.arange(nb), qb))
    return o.swapaxes(0, 1).reshape(B, T, NSA_HEADS, NSA_HD)


def gather_paged(pool, page_table, new, kpos):
    B = new.shape[0]
    bi = jnp.arange(B)[:, None, None, None, None]
    gi = jnp.arange(NSA_KV_HEADS)[None, :, None, None, None]
    page = page_table[bi, jnp.clip(kpos // PAGE_SIZE, 0, page_table.shape[1] - 1)]
    old = pool[page, kpos % PAGE_SIZE, gi]
    fresh = new[bi, jnp.clip(kpos - PAST_LEN, 0, new.shape[1] - 1), gi]
    return jnp.where((kpos < PAST_LEN)[..., None], old, fresh)


def nsa_combine(zgate, o_cmp, o_slc, o_win):
    B, T = zgate.shape[:2]
    g = jax.nn.sigmoid(zgate.astype(jnp.float32)).reshape(B, T, NSA_HEADS, 3, 1)
    o = g[:, :, :, 0] * o_cmp + g[:, :, :, 1] * o_slc + g[:, :, :, 2] * o_win
    return o.reshape(B, T, NSA_WIDTH)


def mem_attend(x, mem_k, mem_v, w_q, w_o):
    B, T, _ = x.shape
    q = (x @ w_q).reshape(B, T, MEM_HEADS, MEM_HD)
    s = jnp.einsum('bthd,bmhd->bhtm', q, mem_k).astype(jnp.float32) * MEM_HD ** -0.5
    p = jax.nn.softmax(s, axis=-1).astype(mem_v.dtype)
    o = jnp.einsum('bhtm,bmhd->bthd', p, mem_v).reshape(B, T, D_MODEL)
    return o @ w_o


def peer_block(x, w_q, sub_keys, u_tab, v_tab):
    n = x.shape[0]
    qh = (x @ w_q).reshape(n, PEER_HEADS, 2, PEER_KEY_DIM)
    s = jnp.einsum('nhcd,hckd->nhck', qh, sub_keys).astype(jnp.float32)
    s1, i1 = lax.top_k(s[:, :, 0], PEER_TOPK)
    s2, i2 = lax.top_k(s[:, :, 1], PEER_TOPK)
    cand = (s1[..., :, None] + s2[..., None, :]).reshape(n, PEER_HEADS, PEER_TOPK * PEER_TOPK)
    cidx = (i1[..., :, None] * PEER_NKEYS + i2[..., None, :]).reshape(n, PEER_HEADS, PEER_TOPK * PEER_TOPK)
    top_s, j = lax.top_k(cand, PEER_TOPK)
    e = jnp.take_along_axis(cidx, j, axis=-1)
    gate = jax.nn.softmax(top_s, axis=-1)
    h = jax.nn.gelu(jnp.einsum('nd,nhkd->nhk', x, u_tab[e]).astype(jnp.float32))
    return jnp.einsum('nhk,nhkd->nd', (gate * h).astype(x.dtype), v_tab[e])


def peer(x, w_q, sub_keys, u_tab, v_tab):
    B, T, D = x.shape
    n = B * T
    nb = -(-n // PEER_TOKEN_BLOCK)
    xb = jnp.pad(x.reshape(n, D), ((0, nb * PEER_TOKEN_BLOCK - n), (0, 0))).reshape(nb, PEER_TOKEN_BLOCK, D)
    y = lax.map(lambda xx: peer_block(xx, w_q, sub_keys, u_tab, v_tab), xb)
    return y.reshape(nb * PEER_TOKEN_BLOCK, D)[:n].reshape(B, T, D)


def trunk_tail(x, o_h, o_n, mem_k, mem_v, p):
    mix = jnp.concatenate([o_h, o_n], axis=-1).astype(x.dtype) @ p['w_out']
    x = layer_norm(DN_ALPHA * x + mix, p['ln1_g'], p['ln1_b'])
    x = layer_norm(DN_ALPHA * x + mem_attend(x, mem_k, mem_v, p['w_mem_q'], p['w_mem_o']), p['ln2_g'], p['ln2_b'])
    x = layer_norm(DN_ALPHA * x + peer(x, p['peer_w_q'], p['peer_sub_keys'], p['peer_u'], p['peer_v']),
                   p['ln3_g'], p['ln3_b'])
    return x


def prompt_layer(x, mem, lb, slopes, p):
    B, T, _ = x.shape
    hq, hf, hi, hg, nq, kc, vc, ks, vs, kw, vw, ng = jnp.split(x @ p['w_in'], IN_OFFSETS, axis=-1)
    q_h, k_h, v_h, lf_h = hgrn_features(hq, hf, hi, lb)
    s0 = jnp.zeros((B, HG_HEADS, HG_DK, HG_DV), jnp.float32)
    o_h, s_h = hgrn_recurrence(q_h, k_h, v_h, lf_h, s0, HG_CHUNK)
    o_h = hgrn_readout(o_h, hg, p['hgrn_norm_g'])
    q = nq.reshape(B, T, NSA_HEADS, NSA_HD)
    kc, vc, ks, vs, kw, vw = (a.reshape(B, T, NSA_KV_HEADS, NSA_HD) for a in (kc, vc, ks, vs, kw, vw))
    pos = jnp.arange(T)
    kc_blk = cmp_finish(cmp_partials(kc, p['cmp_k_w1']), p['cmp_k_b1'], p['cmp_k_w2'], p['cmp_k_b2'])
    vc_blk = cmp_finish(cmp_partials(vc, p['cmp_v_w1']), p['cmp_v_b1'], p['cmp_v_w2'], p['cmp_v_b2'])
    o_c, p_c = nsa_cmp_attend(q, pos, kc_blk, vc_blk, slopes)
    idx = nsa_select(p_c, pos, -(-T // SEL_BLOCK))
    o_s = prompt_slc(q, pos, idx, ks, vs, slopes)
    o_w = prompt_win(q, kw, vw, slopes)
    o_n = nsa_combine(ng, o_c, o_s, o_w)
    mem_k = (mem @ p['w_mem_k']).reshape(B, N_MEM, MEM_HEADS, MEM_HD)
    mem_v = (mem @ p['w_mem_v']).reshape(B, N_MEM, MEM_HEADS, MEM_HD)
    y = trunk_tail(x, o_h, o_n, mem_k, mem_v, p)
    n_win = min(WINDOW, T)
    return y, (kc, vc, ks, vs, kw[:, T - n_win:], vw[:, T - n_win:], s_h, mem_k, mem_v)


def sample_layer(x, pool_kc, pool_vc, pool_ks, pool_vs, win_k, win_v, s_prev, mem_k, mem_v,
                 page_table, lb, slopes, p):
    B, T, _ = x.shape
    hq, hf, hi, hg, nq, kc, vc, ks, vs, kw, vw, ng = jnp.split(x @ p['w_in'], IN_OFFSETS, axis=-1)
    q_h, k_h, v_h, lf_h = hgrn_features(hq, hf, hi, lb)
    o_h, s_h = hgrn_recurrence(q_h, k_h, v_h, lf_h, s_prev.astype(jnp.float32), T)
    o_h = hgrn_readout(o_h, hg, p['hgrn_norm_g'])
    q = nq.reshape(B, T, NSA_HEADS, NSA_HD)
    kc, vc, ks, vs, kw, vw = (a.reshape(B, T, NSA_KV_HEADS, NSA_HD) for a in (kc, vc, ks, vs, kw, vw))
    pos = PAST_LEN + jnp.arange(T)
    n_pages = page_table.shape[1]

    def cmp_full(pool, new, w1, b1, w2, b2):
        past = pool[page_table].reshape(B, n_pages * PAGE_SIZE, NSA_KV_HEADS, NSA_HD)
        a = jnp.concatenate([cmp_partials(past, w1), cmp_partials(new, w1)], axis=2)
        return cmp_finish(a, b1, w2, b2)

    kc_blk = cmp_full(pool_kc, kc, p['cmp_k_w1'], p['cmp_k_b1'], p['cmp_k_w2'], p['cmp_k_b2'])
    vc_blk = cmp_full(pool_vc, vc, p['cmp_v_w1'], p['cmp_v_b1'], p['cmp_v_w2'], p['cmp_v_b2'])
    o_c, p_c = nsa_cmp_attend(q, pos, kc_blk, vc_blk, slopes)
    idx = nsa_select(p_c, pos, -(-(PAST_LEN + T) // SEL_BLOCK))
    kpos = idx[..., None] * SEL_BLOCK + jnp.arange(SEL_BLOCK)
    o_s = nsa_slc_attend(q, pos, kpos, gather_paged(pool_ks, page_table, ks, kpos),
                         gather_paged(pool_vs, page_table, vs, kpos), slopes)
    n_buf = win_k.shape[1]
    kpos_w = PAST_LEN - n_buf + jnp.arange(n_buf + T)
    o_w = nsa_win_attend(q, pos, jnp.concatenate([win_k, kw], axis=1),
                         jnp.concatenate([win_v, vw], axis=1), kpos_w, slopes)
    o_n = nsa_combine(ng, o_c, o_s, o_w)
    y = trunk_tail(x, o_h, o_n, mem_k, mem_v, p)
    return y, (kc, vc, ks, vs, kw, vw, s_h)


def setup_inputs(seed: int = 0) -> dict:
    key = jax.random.key(seed)
    ks = list(jax.random.split(key, 48))

    def nrm(shape, scale=1.0):
        return jax.random.normal(ks.pop(), shape, jnp.float32) * scale

    L = DEPTH
    n_pages = PAST_LEN // PAGE_SIZE
    n_phys = (5 * DEC_BATCH * n_pages) // 4
    win_buf = min(WINDOW, PAST_LEN)
    pool_shape = (L, n_phys, PAGE_SIZE, NSA_KV_HEADS, NSA_HD)
    win_shape = (L, DEC_BATCH, win_buf, NSA_KV_HEADS, NSA_HD)
    memc_shape = (L, DEC_BATCH, N_MEM, MEM_HEADS, MEM_HD)
    col_scale = jnp.concatenate([jnp.full((n,), DN_BETA if v else 1.0, jnp.float32)
                                 for n, v in zip(IN_SIZES, IN_IS_VALUE)])
    d_inv = D_MODEL ** -0.5
    inputs = {
        'x_prompt': nrm((BATCH, SEQ, D_MODEL)),
        'x_sample': nrm((DEC_BATCH, DEC_SEQ, D_MODEL)),
        'cache_k_cmp': nrm(pool_shape),
        'cache_v_cmp': nrm(pool_shape, DN_BETA),
        'cache_k_slc': nrm(pool_shape),
        'cache_v_slc': nrm(pool_shape, DN_BETA),
        'cache_k_win': nrm(win_shape),
        'cache_v_win': nrm(win_shape, DN_BETA),
        'state_hgrn': nrm((L, DEC_BATCH, HG_HEADS, HG_DK, HG_DV), 0.5),
        'cache_mem_k': nrm(memc_shape),
        'cache_mem_v': nrm(memc_shape, DN_BETA),
        'page_table': jax.random.permutation(ks.pop(), n_phys)[:DEC_BATCH * n_pages]
                         .reshape(DEC_BATCH, n_pages).astype(jnp.int32),
        'mem_prompt': nrm((BATCH, N_MEM, D_MODEL)),
        'w_in': nrm((L, D_MODEL, IN_COLS), d_inv) * col_scale,
        'hgrn_lb_logits': nrm((DEPTH + 1, HG_HEADS * HG_DK)),
        'hgrn_norm_g': 1.0 + nrm((L, HG_DV), 0.02),
        'cmp_k_w1': nrm((L, CMP_BLOCK, NSA_HD, CMP_HIDDEN), (CMP_BLOCK * NSA_HD) ** -0.5),
        'cmp_k_b1': nrm((L, CMP_HIDDEN), 0.02),
        'cmp_k_w2': nrm((L, CMP_HIDDEN, NSA_HD), CMP_HIDDEN ** -0.5),
        'cmp_k_b2': nrm((L, NSA_HD), 0.02),
        'cmp_v_w1': nrm((L, CMP_BLOCK, NSA_HD, CMP_HIDDEN), (CMP_BLOCK * NSA_HD) ** -0.5),
        'cmp_v_b1': nrm((L, CMP_HIDDEN), 0.02),
        'cmp_v_w2': nrm((L, CMP_HIDDEN, NSA_HD), CMP_HIDDEN ** -0.5),
        'cmp_v_b2': nrm((L, NSA_HD), 0.02),
        'w_out': nrm((L, MIX_WIDTH, D_MODEL), MIX_WIDTH ** -0.5 * DN_BETA),
        'ln1_g': 1.0 + nrm((L, D_MODEL), 0.02),
        'ln1_b': nrm((L, D_MODEL), 0.02),
        'w_mem_q': nrm((L, D_MODEL, D_MODEL), d_inv),
        'w_mem_k': nrm((L, D_MODEL, D_MODEL), d_inv),
        'w_mem_v': nrm((L, D_MODEL, D_MODEL), d_inv * DN_BETA),
        'w_mem_o': nrm((L, D_MODEL, D_MODEL), d_inv * DN_BETA),
        'ln2_g': 1.0 + nrm((L, D_MODEL), 0.02),
        'ln2_b': nrm((L, D_MODEL), 0.02),
        'peer_w_q': nrm((L, D_MODEL, PEER_HEADS * 2 * PEER_KEY_DIM), d_inv),
        'peer_sub_keys': nrm((L, PEER_HEADS, 2, PEER_NKEYS, PEER_KEY_DIM), PEER_KEY_DIM ** -0.5),
        'peer_u': nrm((L, PEER_EXPERTS, D_MODEL), d_inv),
        'peer_v': nrm((L, PEER_EXPERTS, D_MODEL), DN_BETA),
        'ln3_g': 1.0 + nrm((L, D_MODEL), 0.02),
        'ln3_b': nrm((L, D_MODEL), 0.02),
    }
    return inputs


def reference(x_prompt, x_sample, cache_k_cmp, cache_v_cmp, cache_k_slc, cache_v_slc, cache_k_win, cache_v_win,
              state_hgrn, cache_mem_k, cache_mem_v, page_table, mem_prompt, w_in, hgrn_lb_logits, hgrn_norm_g,
              cmp_k_w1, cmp_k_b1, cmp_k_w2, cmp_k_b2, cmp_v_w1, cmp_v_b1, cmp_v_w2, cmp_v_b2, w_out, ln1_g, ln1_b,
              w_mem_q, w_mem_k, w_mem_v, w_mem_o, ln2_g, ln2_b, peer_w_q, peer_sub_keys, peer_u, peer_v,
              ln3_g, ln3_b):
    slopes = alibi_slopes()
    lb_all = jnp.cumsum(jax.nn.softmax(hgrn_lb_logits.astype(jnp.float32), axis=0), axis=0)
    y_prompt, y_sample = x_prompt, x_sample
    new_p, new_s = [], []
    for layer in range(DEPTH):
        p = {'w_in': w_in[layer], 'hgrn_norm_g': hgrn_norm_g[layer],
             'cmp_k_w1': cmp_k_w1[layer], 'cmp_k_b1': cmp_k_b1[layer],
             'cmp_k_w2': cmp_k_w2[layer], 'cmp_k_b2': cmp_k_b2[layer],
             'cmp_v_w1': cmp_v_w1[layer], 'cmp_v_b1': cmp_v_b1[layer],
             'cmp_v_w2': cmp_v_w2[layer], 'cmp_v_b2': cmp_v_b2[layer],
             'w_out': w_out[layer], 'ln1_g': ln1_g[layer], 'ln1_b': ln1_b[layer],
             'w_mem_q': w_mem_q[layer], 'w_mem_k': w_mem_k[layer], 'w_mem_v': w_mem_v[layer],
             'w_mem_o': w_mem_o[layer], 'ln2_g': ln2_g[layer], 'ln2_b': ln2_b[layer],
             'peer_w_q': peer_w_q[layer], 'peer_sub_keys': peer_sub_keys[layer],
             'peer_u': peer_u[layer], 'peer_v': peer_v[layer], 'ln3_g': ln3_g[layer], 'ln3_b': ln3_b[layer]}
        y_prompt, st_p = prompt_layer(y_prompt, mem_prompt, lb_all[layer], slopes, p)
        y_sample, st_s = sample_layer(y_sample, cache_k_cmp[layer], cache_v_cmp[layer], cache_k_slc[layer],
                                      cache_v_slc[layer], cache_k_win[layer], cache_v_win[layer],
                                      state_hgrn[layer], cache_mem_k[layer], cache_mem_v[layer],
                                      page_table, lb_all[layer], slopes, p)
        new_p.append(st_p)
        new_s.append(st_s)
    (p_k_cmp, p_v_cmp, p_k_slc, p_v_slc, p_k_win, p_v_win, p_state_hgrn, p_mem_k, p_mem_v) = (
        jnp.stack(a) for a in zip(*new_p))
    (s_k_cmp, s_v_cmp, s_k_slc, s_v_slc, s_k_win, s_v_win, s_state_hgrn) = (
        jnp.stack(a) for a in zip(*new_s))
    return (y_prompt, y_sample, p_k_cmp, p_v_cmp, p_k_slc, p_v_slc, p_k_win, p_v_win, p_state_hgrn,
            p_mem_k, p_mem_v, s_k_cmp, s_v_cmp, s_k_slc, s_v_slc, s_k_win, s_v_win, s_state_hgrn)
```

```python
import functools

import numpy as np
import jax
import jax.numpy as jnp
from jax import lax
from jax.experimental import pallas as pl
from jax.experimental.pallas import tpu as pltpu

F32 = jnp.float32
BF16 = jnp.bfloat16

D_MODEL = 1024
HG_HEADS = 4
HG_D = 128
HG_WIDTH = HG_HEADS * HG_D
NSA_HEADS = 8
NSA_GROUPS = 2
NSA_HPG = NSA_HEADS // NSA_GROUPS
NSA_HD = 64
NSA_WIDTH = NSA_HEADS * NSA_HD
KV_WIDTH = NSA_GROUPS * NSA_HD
CMP_BLOCK = 32
CMP_STRIDE = 16
CMP_HIDDEN = 128
SEL_BLOCK = 64
N_SEL = 16
WINDOW = 512
PAGE_SIZE = 128
MEM_HEADS = 4
MEM_HD = D_MODEL // MEM_HEADS
PEER_HEADS = 8
PEER_NKEYS = 128
PEER_KEY_DIM = 128
PEER_TOPK = 16
DN_ALPHA = 2.0 ** 0.25
LN_EPS = 1e-5
NEG_INF = -1e30
ALIBI_SLOPES = tuple(2.0 ** (-8.0 * (h + 1) / NSA_HEADS) for h in range(NSA_HEADS))

LANES = 128
SUBLANES = 8
VMEM_LIMIT_BYTES = 56 * 1024 * 1024

HG_CHUNK = 32
HG_STEP_TOKENS = 256
NSA_Q_TILE = 128
ROW_TILE = 512
PEER_EXPERT_BLOCK = 1024

IN_GROUP_WIDTHS = (4 * HG_WIDTH, NSA_WIDTH) + (KV_WIDTH,) * 6 + (LANES,)
IN_COLS = 4 * HG_WIDTH + NSA_WIDTH + 6 * KV_WIDTH + 3 * NSA_HEADS


def _params(*sem):
    return pltpu.CompilerParams(dimension_semantics=sem, vmem_limit_bytes=VMEM_LIMIT_BYTES)


def _gelu(x):
    return 0.5 * x * (1.0 + jnp.tanh(0.7978845608028654 * (x + 0.044715 * (x * x * x))))


def _layer_norm(y, g, b):
    mu = jnp.mean(y, -1, keepdims=True)
    yc = y - mu
    var = jnp.mean(yc * yc, -1, keepdims=True)
    return yc * lax.rsqrt(var + LN_EPS) * g + b


def _dot(a, b):
    return jnp.dot(a.astype(BF16), b.astype(BF16), preferred_element_type=F32)


def _dot_nt(a, b):
    return lax.dot_general(a.astype(BF16), b.astype(BF16), (((1,), (1,)), ((), ())),
                           preferred_element_type=F32)


def _dot_tn(a, b):
    return lax.dot_general(a.astype(BF16), b.astype(BF16), (((0,), (0,)), ((), ())),
                           preferred_element_type=F32)


def _masked_exp(s, mask):
    sm = jnp.where(mask, s, NEG_INF)
    m = jnp.max(sm, -1, keepdims=True)
    e = jnp.where(mask, jnp.exp(sm - m), 0.0)
    return e, jnp.sum(e, -1, keepdims=True)


def _safe_inv(d):
    return jnp.where(d > 0.0, 1.0 / jnp.where(d > 0.0, d, 1.0), 0.0)


def _inproj_kernel(x_ref, w_ref, *out_refs):
    xb = x_ref[...].astype(BF16)
    off = 0
    for o_ref in out_refs:
        width = o_ref.shape[-1]
        o_ref[...] = jnp.dot(xb, w_ref[:, off:off + width], preferred_element_type=F32)
        off += width


def _inproj(x, w_pad):
    n = x.shape[0]
    tm = min(ROW_TILE, n)
    return pl.pallas_call(
        _inproj_kernel,
        grid=(n // tm,),
        in_specs=[pl.BlockSpec((tm, D_MODEL), lambda i: (i, 0)),
                  pl.BlockSpec(w_pad.shape, lambda i: (0, 0))],
        out_specs=[pl.BlockSpec((tm, w), lambda i: (i, 0)) for w in IN_GROUP_WIDTHS],
        out_shape=[jax.ShapeDtypeStruct((n, w), F32) for w in IN_GROUP_WIDTHS],
        compiler_params=_params("parallel"),
        name="inproj",
    )(x, w_pad)


def _hgrn_kernel(zh_ref, lbl_ref, ng_ref, s0_ref, tri_ref, o_ref, sout_ref, st_ref, obuf_ref,
                 *, chunk, t_valid):
    ti = pl.program_id(1)
    step_tokens = zh_ref.shape[1]

    @pl.when(ti == 0)
    def _():
        for h in range(HG_HEADS):
            st_ref[h] = s0_ref[0, h].T

    logits = lbl_ref[...]
    le = jnp.exp(logits - jnp.max(logits, axis=0, keepdims=True))
    lb_all = le[0:1] / jnp.sum(le, axis=0, keepdims=True)
    norm_g = ng_ref[...]
    row = lax.broadcasted_iota(jnp.int32, (chunk, 1), 0)

    def do_chunk(c, carry):
        r0 = pl.multiple_of(c * chunk, chunk)
        rows = pl.ds(r0, chunk)
        for h in range(HG_HEADS):
            cs = h * HG_D
            q = zh_ref[0, rows, cs:cs + HG_D]
            zf = zh_ref[0, rows, HG_WIDTH + cs:HG_WIDTH + cs + HG_D]
            v = zh_ref[0, rows, 2 * HG_WIDTH + cs:2 * HG_WIDTH + cs + HG_D]
            zg = zh_ref[0, rows, 3 * HG_WIDTH + cs:3 * HG_WIDTH + cs + HG_D]
            lb = lb_all[:, cs:cs + HG_D]
            lf = jnp.log(lb + (1.0 - lb) * jax.nn.sigmoid(zf))
            k = (1.0 - lb) * jax.nn.sigmoid(-zf)
            if t_valid < chunk:
                lf = jnp.where(row < t_valid, lf, 0.0)
                k = jnp.where(row < t_valid, k, 0.0)
            if chunk <= SUBLANES:
                g = jnp.where(row >= 0, lf[0:1], 0.0)
                for t in range(1, chunk):
                    g = g + jnp.where(row >= t, lf[t:t + 1], 0.0)
            else:
                g = jnp.dot(tri_ref[...], lf, precision=lax.Precision.HIGHEST, preferred_element_type=F32)
            st = st_ref[h]
            o_inter = _dot_nt(q * jnp.exp(g), st)
            for t in range(chunk):
                r8 = -(-(t + 1) // SUBLANES) * SUBLANES
                rel = jnp.where(row[:r8] <= t, g[t:t + 1] - g[:r8], NEG_INF)
                p = (q[t:t + 1] * k[:r8]) * jnp.exp(rel)
                w = jnp.sum(p, axis=-1, keepdims=True)
                obuf_ref[t:t + 1, :] = jnp.sum(w * v[:r8], axis=0, keepdims=True)
            o = o_inter + obuf_ref[...]
            g_end = g[chunk - 1:chunk]
            kk = k * jnp.exp(g_end - g)
            st_ref[h] = st * jnp.exp(g_end) + _dot_tn(v, kk)
            o = o * lax.rsqrt(jnp.mean(o * o, -1, keepdims=True) + LN_EPS) * norm_g
            o_ref[0, rows, cs:cs + HG_D] = o * jax.nn.sigmoid(zg)
        return carry

    lax.fori_loop(0, step_tokens // chunk, do_chunk, 0)

    @pl.when(ti == pl.num_programs(1) - 1)
    def _():
        for h in range(HG_HEADS):
            sout_ref[0, h] = st_ref[h].T


def _hgrn(zh, lb_logits, norm_g, s0, *, chunk, step_tokens, t_valid):
    b, t, _ = zh.shape
    tri = jnp.asarray(np.tril(np.ones((chunk, chunk), np.float32)))
    kern = functools.partial(_hgrn_kernel, chunk=chunk, t_valid=t_valid)
    return pl.pallas_call(
        kern,
        grid=(b, t // step_tokens),
        in_specs=[pl.BlockSpec((1, step_tokens, 4 * HG_WIDTH), lambda i, j: (i, j, 0)),
                  pl.BlockSpec(lb_logits.shape, lambda i, j: (0, 0)),
                  pl.BlockSpec((1, HG_D), lambda i, j: (0, 0)),
                  pl.BlockSpec((1, HG_HEADS, HG_D, HG_D), lambda i, j: (i, 0, 0, 0)),
                  pl.BlockSpec((chunk, chunk), lambda i, j: (0, 0))],
        out_specs=[pl.BlockSpec((1, step_tokens, HG_WIDTH), lambda i, j: (i, j, 0)),
                   pl.BlockSpec((1, HG_HEADS, HG_D, HG_D), lambda i, j: (i, 0, 0, 0))],
        out_shape=[jax.ShapeDtypeStruct((b, t, HG_WIDTH), F32),
                   jax.ShapeDtypeStruct((b, HG_HEADS, HG_D, HG_D), F32)],
        scratch_shapes=[pltpu.VMEM((HG_HEADS, HG_D, HG_D), F32), pltpu.VMEM((chunk, HG_D), F32)],
        compiler_params=_params("parallel", "arbitrary"),
        name="hgrn",
    )(zh, lb_logits, norm_g.reshape(1, HG_D), s0, tri)


def _cmp_blocks(view, w1_ref, b1_ref, w2_ref, b2_ref):
    n_sub = view.shape[0]
    a = _dot(view, w1_ref[...])
    hid = []
    for g in range(NSA_GROUPS):
        a0 = a[:, (2 * g) * CMP_HIDDEN:(2 * g + 1) * CMP_HIDDEN]
        a1 = a[:, (2 * g + 1) * CMP_HIDDEN:(2 * g + 2) * CMP_HIDDEN]
        hid.append(a0 + pltpu.roll(a1, n_sub - 1, axis=0))
    h = jnp.concatenate(hid, axis=-1) + b1_ref[...]
    return _dot(_gelu(h), w2_ref[...]) + b2_ref[...]


def _cmp_kernel(kv_ref, vv_ref, kw1, kb1, kw2, kb2, vw1, vb1, vw2, vb2, ko_ref, vo_ref):
    ko_ref[0] = _cmp_blocks(kv_ref[0], kw1, kb1, kw2, kb2)
    vo_ref[0] = _cmp_blocks(vv_ref[0], vw1, vb1, vw2, vb2)


def _cmp_weights(w1, b1, w2, b2):
    w1r = w1.reshape(CMP_BLOCK // CMP_STRIDE, CMP_STRIDE, NSA_HD, CMP_HIDDEN)
    big = jnp.zeros((CMP_STRIDE, NSA_GROUPS, NSA_HD, NSA_GROUPS, 2, CMP_HIDDEN), F32)
    for g in range(NSA_GROUPS):
        big = big.at[:, g, :, g, :, :].set(jnp.transpose(w1r, (1, 2, 0, 3)))
    w1b = big.reshape(CMP_STRIDE * KV_WIDTH, NSA_GROUPS * 2 * CMP_HIDDEN).astype(BF16)
    w2b = jnp.zeros((NSA_GROUPS, CMP_HIDDEN, NSA_GROUPS, NSA_HD), F32)
    for g in range(NSA_GROUPS):
        w2b = w2b.at[g, :, g, :].set(w2)
    w2b = w2b.reshape(NSA_GROUPS * CMP_HIDDEN, KV_WIDTH).astype(BF16)
    b1b = jnp.tile(b1, NSA_GROUPS).reshape(1, NSA_GROUPS * CMP_HIDDEN)
    b2b = jnp.tile(b2, NSA_GROUPS).reshape(1, KV_WIDTH)
    return w1b, b1b, w2b, b2b


def _cmp_prompt(kc, vc, kwts, vwts):
    b, t, _ = kc.shape
    n_sub = t // CMP_STRIDE
    view = lambda a: a.reshape(b, n_sub, CMP_STRIDE * KV_WIDTH)
    wspecs = [pl.BlockSpec(w.shape, lambda i: (0, 0)) for w in kwts + vwts]
    return pl.pallas_call(
        _cmp_kernel,
        grid=(b,),
        in_specs=[pl.BlockSpec((1, n_sub, CMP_STRIDE * KV_WIDTH), lambda i: (i, 0, 0))] * 2 + wspecs,
        out_specs=[pl.BlockSpec((1, n_sub, KV_WIDTH), lambda i: (i, 0, 0))] * 2,
        out_shape=[jax.ShapeDtypeStruct((b, n_sub, KV_WIDTH), F32)] * 2,
        compiler_params=_params("parallel"),
        name="cmp_prompt",
    )(view(kc), view(vc), *kwts, *vwts)


def _cmp_to_slc(n_cmp, n_slc):
    cs = np.arange(n_cmp)[:, None] * CMP_STRIDE
    ss = np.arange(n_slc)[None, :] * SEL_BLOCK
    shared = np.clip(np.minimum(cs + CMP_BLOCK, ss + SEL_BLOCK) - np.maximum(cs, ss), 0, None)
    return (shared / CMP_STRIDE).astype(np.float32)


def _nsa_prompt_kernel(q_ref, gate_ref, kb_ref, vb_ref, ks_ref, vs_ref, kw_ref, vw_ref, mt_ref, ex_ref,
                       o_ref, *, n_cmp):
    tq = q_ref.shape[1]
    t_all = ks_ref.shape[1]
    n_sub = kb_ref.shape[1]
    n_slc = mt_ref.shape[0]
    t0 = pl.program_id(1) * tq
    q = q_ref[0] * (NSA_HD ** -0.5)
    gates = jax.nn.sigmoid(gate_ref[0])
    qpos = t0 + lax.broadcasted_iota(jnp.int32, (tq, 1), 0)

    cidx = lax.broadcasted_iota(jnp.int32, (1, n_sub), 1)
    dist_c = qpos - (cidx * CMP_STRIDE + CMP_BLOCK - 1)
    mask_c = (dist_c >= 0) & (cidx < n_cmp)
    dist_cf = dist_c.astype(F32)
    nidx = lax.broadcasted_iota(jnp.int32, (n_slc, 1), 0)
    tpos = t0 + lax.broadcasted_iota(jnp.int32, (1, tq), 1)
    elig = nidx * SEL_BLOCK <= tpos
    force = (nidx == tpos // SEL_BLOCK) | (nidx == 0)
    kpos = lax.broadcasted_iota(jnp.int32, (1, t_all), 1)
    dist_s = qpos - kpos
    dist_sf = dist_s.astype(F32)
    span = min(WINDOW + tq, t_all)
    start = pl.multiple_of(jnp.clip(t0 - WINDOW, 0, t_all - span), tq)
    dist_w = qpos - (start + lax.broadcasted_iota(jnp.int32, (1, span), 1))
    mask_w = (dist_w >= 0) & (dist_w < WINDOW)
    dist_wf = dist_w.astype(F32)

    for g in range(NSA_GROUPS):
        gs = g * NSA_HD
        kb = kb_ref[0, :, gs:gs + NSA_HD]
        vb = vb_ref[0, :, gs:gs + NSA_HD]
        o_cmp = []
        psum = jnp.zeros((tq, n_sub), F32)
        for j in range(NSA_HPG):
            h = g * NSA_HPG + j
            qh = q[:, h * NSA_HD:(h + 1) * NSA_HD]
            s = _dot_nt(qh, kb) - ALIBI_SLOPES[h] * dist_cf
            e, d = _masked_exp(s, mask_c)
            p = e * _safe_inv(d)
            psum = psum + p
            o_cmp.append(_dot(p, vb))
        imp_t = lax.dot_general(mt_ref[...], psum, (((1,), (1,)), ((), ())),
                                precision=lax.Precision.HIGHEST, preferred_element_type=F32)
        val = jnp.where(force, jnp.inf, jnp.where(elig, imp_t, -jnp.inf))
        cnt = jnp.zeros((n_slc, tq), F32)
        for m in range(n_slc):
            other = val[m:m + 1]
            tie = jnp.where(nidx > m, 1.0, 0.0)
            cnt = cnt + jnp.where(other > val, 1.0, jnp.where(other == val, tie, 0.0))
        sel_t = jnp.where(cnt < float(min(N_SEL, n_slc)), 1.0, 0.0)
        sel_keys = _dot(sel_t.T, ex_ref[...])
        mask_s = (sel_keys > 0.5) & (dist_s >= 0)

        ks = ks_ref[0, :, gs:gs + NSA_HD]
        vs = vs_ref[0, :, gs:gs + NSA_HD]
        kw = kw_ref[0, pl.ds(start, span), gs:gs + NSA_HD]
        vw = vw_ref[0, pl.ds(start, span), gs:gs + NSA_HD]
        for j in range(NSA_HPG):
            h = g * NSA_HPG + j
            qh = q[:, h * NSA_HD:(h + 1) * NSA_HD]
            e, d = _masked_exp(_dot_nt(qh, ks) - ALIBI_SLOPES[h] * dist_sf, mask_s)
            o_slc = _dot(e, vs) * _safe_inv(d)
            e, d = _masked_exp(_dot_nt(qh, kw) - ALIBI_SLOPES[h] * dist_wf, mask_w)
            o_win = _dot(e, vw) * _safe_inv(d)
            o_ref[0, :, h * NSA_HD:(h + 1) * NSA_HD] = (
                gates[:, 3 * h:3 * h + 1] * o_cmp[j]
                + gates[:, 3 * h + 1:3 * h + 2] * o_slc
                + gates[:, 3 * h + 2:3 * h + 3] * o_win)


def _nsa_prompt(nq, ng, kblk, vblk, ks, vs, kw, vw):
    b, t, _ = nq.shape
    tq = min(NSA_Q_TILE, t)
    n_sub = kblk.shape[1]
    n_cmp = n_sub - CMP_BLOCK // CMP_STRIDE + 1
    n_slc = -(-t // SEL_BLOCK)
    mt = np.zeros((n_slc, n_sub), np.float32)
    mt[:, :n_cmp] = _cmp_to_slc(n_cmp, n_slc).T
    expand = (np.arange(t)[None, :] // SEL_BLOCK == np.arange(n_slc)[:, None]).astype(np.float32)
    full = lambda w: pl.BlockSpec((1, t, w), lambda i, j: (i, 0, 0))
    return pl.pallas_call(
        functools.partial(_nsa_prompt_kernel, n_cmp=n_cmp),
        grid=(b, t // tq),
        in_specs=[pl.BlockSpec((1, tq, NSA_WIDTH), lambda i, j: (i, j, 0)),
                  pl.BlockSpec((1, tq, LANES), lambda i, j: (i, j, 0)),
                  pl.BlockSpec((1, n_sub, KV_WIDTH), lambda i, j: (i, 0, 0)),
                  pl.BlockSpec((1, n_sub, KV_WIDTH), lambda i, j: (i, 0, 0)),
                  full(KV_WIDTH), full(KV_WIDTH), full(KV_WIDTH), full(KV_WIDTH),
                  pl.BlockSpec(mt.shape, lambda i, j: (0, 0)),
                  pl.BlockSpec(expand.shape, lambda i, j: (0, 0))],
        out_specs=pl.BlockSpec((1, tq, NSA_WIDTH), lambda i, j: (i, j, 0)),
        out_shape=jax.ShapeDtypeStruct((b, t, NSA_WIDTH), F32),
        compiler_params=_params("parallel", "arbitrary"),
        name="nsa_prompt",
    )(nq, ng, kblk, vblk, ks, vs, kw, vw, jnp.asarray(mt), jnp.asarray(expand, dtype=BF16))


def _nsa_sample_kernel(pt_ref, q_ref, gate_ref, ksn_ref, vsn_ref, kwn_ref, vwn_ref, wk_ref, wv_ref,
                       kw1, kb1, kw2, kb2, vw1, vb1, vw2, vb2, m_ref, ex_ref,
                       kc_pool, vc_pool, ks_pool, vs_pool,
                       o_ref, kc_buf, vc_buf, ks_buf, vs_buf, sem, *, t_valid, n_slc):
    b = pl.program_id(0)
    n_pages = pt_ref.shape[1]
    tp = q_ref.shape[1]
    past = n_pages * PAGE_SIZE
    sub_per_page = PAGE_SIZE // CMP_STRIDE
    n_sub = past // CMP_STRIDE
    n_cmp = n_sub - CMP_BLOCK // CMP_STRIDE + 1
    n_past_blocks = past // SEL_BLOCK
    rows_q = NSA_HPG * tp

    def page_copies(j):
        page = pt_ref[b, j]
        return (
            pltpu.make_async_copy(kc_pool.at[page], kc_buf.at[pl.ds(j * sub_per_page, sub_per_page)], sem.at[0]),
            pltpu.make_async_copy(vc_pool.at[page], vc_buf.at[pl.ds(j * sub_per_page, sub_per_page)], sem.at[1]),
            pltpu.make_async_copy(ks_pool.at[page], ks_buf.at[pl.ds(j * PAGE_SIZE, PAGE_SIZE)], sem.at[2]),
            pltpu.make_async_copy(vs_pool.at[page], vs_buf.at[pl.ds(j * PAGE_SIZE, PAGE_SIZE)], sem.at[3]),
        )

    def start_page(j, carry):
        for cp in page_copies(j):
            cp.start()
        return carry

    def wait_page(j, carry):
        for cp in page_copies(j):
            cp.wait()
        return carry

    lax.fori_loop(0, n_pages, start_page, 0)
    lax.fori_loop(0, n_pages, wait_page, 0)

    kblk = _cmp_blocks(kc_buf[...], kw1, kb1, kw2, kb2)
    vblk = _cmp_blocks(vc_buf[...], vw1, vb1, vw2, vb2)

    q = q_ref[0] * (NSA_HD ** -0.5)
    gates = jax.nn.sigmoid(gate_ref[0])
    trow = lax.broadcasted_iota(jnp.int32, (tp, 1), 0)
    trow_q = jnp.concatenate([trow] * NSA_HPG, axis=0)
    qpos_q = past + trow_q
    cidx = lax.broadcasted_iota(jnp.int32, (1, n_sub), 1)
    dist_c = qpos_q - (cidx * CMP_STRIDE + CMP_BLOCK - 1)
    mask_c = (dist_c >= 0) & (cidx < n_cmp)
    dist_cf = dist_c.astype(F32)
    n_lane = m_ref.shape[1]
    nidx = lax.broadcasted_iota(jnp.int32, (tp, n_lane), 1)
    qpos_t = past + trow
    elig = (nidx * SEL_BLOCK <= qpos_t) & (nidx < n_slc)
    force = ((nidx == qpos_t // SEL_BLOCK) | (nidx == 0)) & (nidx < n_slc)
    kpos = lax.broadcasted_iota(jnp.int32, (1, past), 1)
    dist_pf = (qpos_q - kpos).astype(F32)
    rnew = lax.broadcasted_iota(jnp.int32, (1, tp), 1)
    dist_n = trow_q - rnew
    mask_n = (dist_n >= 0) & (rnew < t_valid)
    dist_nf = dist_n.astype(F32)
    n_buf = wk_ref.shape[1]
    dist_b = qpos_q - (past - n_buf + lax.broadcasted_iota(jnp.int32, (1, n_buf), 1))
    mask_b = (dist_b >= 0) & (dist_b < WINDOW)
    dist_bf = dist_b.astype(F32)

    for g in range(NSA_GROUPS):
        gs = g * NSA_HD
        heads = range(g * NSA_HPG, (g + 1) * NSA_HPG)
        qg = jnp.concatenate([q[:, h * NSA_HD:(h + 1) * NSA_HD] for h in heads], axis=0)
        slope = jnp.concatenate([jnp.full((tp, 1), ALIBI_SLOPES[h], F32) for h in heads], axis=0)
        s = _dot_nt(qg, kblk[:, gs:gs + NSA_HD]) - slope * dist_cf
        e, d = _masked_exp(s, mask_c)
        p = e * _safe_inv(d)
        o_cmp = _dot(p, vblk[:, gs:gs + NSA_HD])
        psum = p[0:tp]
        for j in range(1, NSA_HPG):
            psum = psum + p[j * tp:(j + 1) * tp]
        imp = jnp.dot(psum, m_ref[...], precision=lax.Precision.HIGHEST, preferred_element_type=F32)
        val = jnp.where(force, jnp.inf, jnp.where(elig, imp, -jnp.inf))
        sel = jnp.zeros((tp, n_lane), F32)
        for _ in range(min(N_SEL, n_slc)):
            m = jnp.max(val, axis=-1, keepdims=True)
            first = jnp.min(jnp.where(val == m, nidx, n_lane), axis=-1, keepdims=True)
            pick = nidx == first
            sel = jnp.where(pick, 1.0, sel)
            val = jnp.where(pick, -jnp.inf, val)
        sel = jnp.where(nidx < n_slc, sel, 0.0)
        sel_keys = _dot(sel[:, :n_past_blocks], ex_ref[...])
        mask_p = jnp.concatenate([sel_keys] * NSA_HPG, axis=0) > 0.5
        sel_new = jnp.concatenate([sel[:, n_past_blocks:n_past_blocks + 1]] * NSA_HPG, axis=0) > 0.5
        s_p = _dot_nt(qg, ks_buf[:, gs:gs + NSA_HD]) - slope * dist_pf
        s_n = _dot_nt(qg, ksn_ref[0, :, gs:gs + NSA_HD]) - slope * dist_nf
        mask_sn = mask_n & sel_new
        mx = jnp.maximum(jnp.max(jnp.where(mask_p, s_p, NEG_INF), -1, keepdims=True),
                         jnp.max(jnp.where(mask_sn, s_n, NEG_INF), -1, keepdims=True))
        e_p = jnp.where(mask_p, jnp.exp(jnp.where(mask_p, s_p, NEG_INF) - mx), 0.0)
        e_n = jnp.where(mask_sn, jnp.exp(jnp.where(mask_sn, s_n, NEG_INF) - mx), 0.0)
        den = jnp.sum(e_p, -1, keepdims=True) + jnp.sum(e_n, -1, keepdims=True)
        o_slc = (_dot(e_p, vs_buf[:, gs:gs + NSA_HD]) + _dot(e_n, vsn_ref[0, :, gs:gs + NSA_HD])) * _safe_inv(den)
        s_b = _dot_nt(qg, wk_ref[0, :, gs:gs + NSA_HD]) - slope * dist_bf
        s_n = _dot_nt(qg, kwn_ref[0, :, gs:gs + NSA_HD]) - slope * dist_nf
        mx = jnp.maximum(jnp.max(jnp.where(mask_b, s_b, NEG_INF), -1, keepdims=True),
                         jnp.max(jnp.where(mask_n, s_n, NEG_INF), -1, keepdims=True))
        e_b = jnp.where(mask_b, jnp.exp(jnp.where(mask_b, s_b, NEG_INF) - mx), 0.0)
        e_n = jnp.where(mask_n, jnp.exp(jnp.where(mask_n, s_n, NEG_INF) - mx), 0.0)
        den = jnp.sum(e_b, -1, keepdims=True) + jnp.sum(e_n, -1, keepdims=True)
        o_win = (_dot(e_b, wv_ref[0, :, gs:gs + NSA_HD]) + _dot(e_n, vwn_ref[0, :, gs:gs + NSA_HD])) * _safe_inv(den)
        for j, h in enumerate(heads):
            rs = slice(j * tp, (j + 1) * tp)
            o_ref[0, :, h * NSA_HD:(h + 1) * NSA_HD] = (
                gates[:, 3 * h:3 * h + 1] * o_cmp[rs]
                + gates[:, 3 * h + 1:3 * h + 2] * o_slc[rs]
                + gates[:, 3 * h + 2:3 * h + 3] * o_win[rs])


def _nsa_sample(page_table, nq, ng, ksn, vsn, kwn, vwn, win_k, win_v, kwts, vwts,
                pool_kc, pool_vc, pool_ks, pool_vs, *, t_valid):
    db, tp, _ = nq.shape
    n_pages = page_table.shape[1]
    past = n_pages * PAGE_SIZE
    n_phys = pool_kc.shape[0]
    n_sub = past // CMP_STRIDE
    n_cmp = n_sub - CMP_BLOCK // CMP_STRIDE + 1
    n_slc = -(-(past + t_valid) // SEL_BLOCK)
    n_lane = -(-n_slc // LANES) * LANES
    n_buf = win_k.shape[1]
    m = np.zeros((n_sub, n_lane), np.float32)
    m[:n_cmp, :n_slc] = _cmp_to_slc(n_cmp, n_slc)
    n_past_blocks = past // SEL_BLOCK
    expand = (np.arange(past)[None, :] // SEL_BLOCK == np.arange(n_past_blocks)[:, None]).astype(np.float32)
    sub_per_page = PAGE_SIZE // CMP_STRIDE
    stride_view = lambda p: p.reshape(n_phys, sub_per_page, CMP_STRIDE * KV_WIDTH)
    row_view = lambda p: p.reshape(n_phys, PAGE_SIZE, KV_WIDTH)
    per_b = lambda r, w: pl.BlockSpec((1, r, w), lambda i, pt: (i, 0, 0))
    const = lambda a: pl.BlockSpec(a.shape, lambda i, pt: (0,) * a.ndim)
    wts = kwts + vwts
    m_j = jnp.asarray(m)
    ex_j = jnp.asarray(expand, dtype=BF16)
    any_spec = pl.BlockSpec(memory_space=pl.ANY)
    grid_spec = pltpu.PrefetchScalarGridSpec(
        num_scalar_prefetch=1,
        grid=(db,),
        in_specs=[per_b(tp, NSA_WIDTH), per_b(tp, LANES)] + [per_b(tp, KV_WIDTH)] * 4
                 + [per_b(n_buf, KV_WIDTH)] * 2 + [const(w) for w in wts] + [const(m_j), const(ex_j)]
                 + [any_spec] * 4,
        out_specs=per_b(tp, NSA_WIDTH),
        scratch_shapes=[pltpu.VMEM((n_sub, CMP_STRIDE * KV_WIDTH), F32)] * 2
                       + [pltpu.VMEM((past, KV_WIDTH), F32)] * 2
                       + [pltpu.SemaphoreType.DMA((4,))],
    )
    return pl.pallas_call(
        functools.partial(_nsa_sample_kernel, t_valid=t_valid, n_slc=n_slc),
        grid_spec=grid_spec,
        out_shape=jax.ShapeDtypeStruct((db, tp, NSA_WIDTH), F32),
        compiler_params=_params("arbitrary"),
        name="nsa_sample",
    )(page_table, nq, ng, ksn, vsn, kwn, vwn, win_k, win_v, *wts, m_j, ex_j,
      stride_view(pool_kc), stride_view(pool_vc), row_view(pool_ks), row_view(pool_vs))


def _outproj_kernel(oh_ref, on_ref, x_ref, w_ref, g_ref, b_ref, y_ref):
    mix = (jnp.dot(oh_ref[...].astype(BF16), w_ref[:HG_WIDTH, :], preferred_element_type=F32)
           + jnp.dot(on_ref[...].astype(BF16), w_ref[HG_WIDTH:, :], preferred_element_type=F32))
    y_ref[...] = _layer_norm(DN_ALPHA * x_ref[...] + mix, g_ref[...], b_ref[...])


def _outproj(o_h, o_n, x, w_bf, g, b):
    n = x.shape[0]
    tm = min(ROW_TILE, n)
    row = lambda w: pl.BlockSpec((tm, w), lambda i: (i, 0))
    const = lambda a: pl.BlockSpec(a.shape, lambda i: (0, 0))
    return pl.pallas_call(
        _outproj_kernel,
        grid=(n // tm,),
        in_specs=[row(HG_WIDTH), row(NSA_WIDTH), row(D_MODEL), const(w_bf), const(g), const(b)],
        out_specs=row(D_MODEL),
        out_shape=jax.ShapeDtypeStruct((n, D_MODEL), F32),
        compiler_params=_params("parallel"),
        name="outproj_ln1",
    )(o_h, o_n, x, w_bf, g, b)


def _memkv_kernel(m_ref, wk_ref, wv_ref, k_ref, v_ref):
    mb = m_ref[...].astype(BF16)
    k_ref[...] = jnp.dot(mb, wk_ref[...], preferred_element_type=F32)
    v_ref[...] = jnp.dot(mb, wv_ref[...], preferred_element_type=F32)


def _memkv(mem, wk_bf, wv_bf):
    n = mem.shape[0]
    tm = min(ROW_TILE, n)
    row = pl.BlockSpec((tm, D_MODEL), lambda i: (i, 0))
    const = lambda a: pl.BlockSpec(a.shape, lambda i: (0, 0))
    return pl.pallas_call(
        _memkv_kernel,
        grid=(n // tm,),
        in_specs=[row, const(wk_bf), const(wv_bf)],
        out_specs=[row, row],
        out_shape=[jax.ShapeDtypeStruct((n, D_MODEL), F32)] * 2,
        compiler_params=_params("parallel"),
        name="mem_kv",
    )(mem, wk_bf, wv_bf)


def _memattn_kernel(x_ref, mk_ref, mv_ref, wq_ref, wo_ref, g_ref, b_ref, y_ref, o_buf):
    nb, tm, _ = x_ref.shape
    x = x_ref[...].reshape(nb * tm, D_MODEL)
    q = jnp.dot(x.astype(BF16), wq_ref[...], preferred_element_type=F32) * (MEM_HD ** -0.5)
    for i in range(nb):
        rs = slice(i * tm, (i + 1) * tm)
        for h in range(MEM_HEADS):
            cs = slice(h * MEM_HD, (h + 1) * MEM_HD)
            s = _dot_nt(q[rs, cs], mk_ref[i, :, cs])
            e = jnp.exp(s - jnp.max(s, -1, keepdims=True))
            o_buf[rs, cs] = _dot(e, mv_ref[i, :, cs]) / jnp.sum(e, -1, keepdims=True)
    att = jnp.dot(o_buf[...].astype(BF16), wo_ref[...], preferred_element_type=F32)
    y = _layer_norm(DN_ALPHA * x + att, g_ref[...], b_ref[...])
    y_ref[...] = y.reshape(nb, tm, D_MODEL)


def _memattn(x, mem_k, mem_v, wq_bf, wo_bf, g, b, *, nb, tm):
    bsz, t, _ = x.shape
    n_mem = mem_k.shape[1]
    const = lambda a: pl.BlockSpec(a.shape, lambda i, j: (0, 0))
    return pl.pallas_call(
        _memattn_kernel,
        grid=(bsz // nb, t // tm),
        in_specs=[pl.BlockSpec((nb, tm, D_MODEL), lambda i, j: (i, j, 0)),
                  pl.BlockSpec((nb, n_mem, D_MODEL), lambda i, j: (i, 0, 0)),
                  pl.BlockSpec((nb, n_mem, D_MODEL), lambda i, j: (i, 0, 0)),
                  const(wq_bf), const(wo_bf), const(g), const(b)],
        out_specs=pl.BlockSpec((nb, tm, D_MODEL), lambda i, j: (i, j, 0)),
        out_shape=jax.ShapeDtypeStruct((bsz, t, D_MODEL), F32),
        scratch_shapes=[pltpu.VMEM((nb * tm, D_MODEL), F32)],
        compiler_params=_params("parallel", "arbitrary"),
        name="mem_attn_ln2",
    )(x, mem_k, mem_v, wq_bf, wo_bf, g, b)


def _top16_rows(s):
    n = s.shape[0]
    kidx = lax.broadcasted_iota(jnp.int32, s.shape, 0)
    rank = jnp.full(s.shape, float(PEER_TOPK), F32)
    tops = []
    for i in range(PEER_TOPK):
        m = jnp.max(s, axis=0, keepdims=True)
        first = jnp.min(jnp.where(s == m, kidx, n), axis=0, keepdims=True)
        hit = kidx == first
        rank = jnp.where(hit, float(i), rank)
        s = jnp.where(hit, -jnp.inf, s)
        tops.append(m)
    return rank, tops


def _peer_kernel(x_ref, wq_ref, sk_ref, u_ref, vt_ref, g_ref, b_ref, y_ref,
                 xb_ref, lr_ref, e1_ref, rank2_ref, e2_ref, acc_ref):
    kb = pl.program_id(1)
    tm = x_ref.shape[0]
    eb = u_ref.shape[0]
    a_per_block = eb // PEER_NKEYS

    @pl.when(kb == 0)
    def _():
        xb = x_ref[...].astype(BF16)
        xb_ref[...] = xb
        acc_ref[...] = jnp.zeros_like(acc_ref)
        qh = jnp.dot(xb, wq_ref[...], preferred_element_type=F32)
        for h in range(PEER_HEADS):
            c0 = (2 * h) * PEER_KEY_DIM
            s1 = _dot_nt(sk_ref[h, 0], qh[:, c0:c0 + PEER_KEY_DIM])
            s2 = _dot_nt(sk_ref[h, 1], qh[:, c0 + PEER_KEY_DIM:c0 + 2 * PEER_KEY_DIM])
            rank1, top1 = _top16_rows(s1)
            rank2, top2 = _top16_rows(s2)
            t2 = jnp.concatenate(top2, axis=0)
            cand = jnp.concatenate([top1[i] + t2 for i in range(PEER_TOPK)], axis=0)
            crank, ctops = _top16_rows(cand)
            chosen = crank < float(PEER_TOPK)
            z = jnp.sum(jnp.where(chosen, jnp.exp(cand - ctops[0]), 0.0), axis=0, keepdims=True)
            cnt = jnp.where(chosen, 1.0, 0.0)
            lrow = jnp.zeros(s1.shape, F32)
            for i in range(PEER_TOPK):
                l_i = jnp.sum(cnt[i * PEER_TOPK:(i + 1) * PEER_TOPK], axis=0, keepdims=True)
                lrow = lrow + jnp.where(rank1 == float(i), l_i, 0.0)
            lr_ref[h] = lrow
            e1_ref[h] = jnp.exp(s1 - top1[0]) / z
            rank2_ref[h] = rank2.astype(BF16)
            e2_ref[h] = jnp.exp(s2 - top2[0]).astype(BF16)

    ht = _dot_nt(u_ref[...], xb_ref[...])
    gh = _gelu(ht).astype(BF16)
    wh = []
    for al in range(a_per_block):
        a = kb * a_per_block + al
        w = jnp.zeros((PEER_NKEYS, tm), BF16)
        for h in range(PEER_HEADS):
            lr = jnp.broadcast_to(lr_ref[h, pl.ds(a, 1), :].astype(BF16), (PEER_NKEYS, tm))
            e1 = jnp.broadcast_to(e1_ref[h, pl.ds(a, 1), :].astype(BF16), (PEER_NKEYS, tm))
            w = w + jnp.where(rank2_ref[h] < lr, e2_ref[h], jnp.zeros_like(w)) * e1
        wh.append(w * gh[al * PEER_NKEYS:(al + 1) * PEER_NKEYS])
    acc_ref[...] += jnp.dot(vt_ref[...], jnp.concatenate(wh, axis=0), preferred_element_type=F32)

    @pl.when(kb == pl.num_programs(1) - 1)
    def _():
        y_ref[...] = _layer_norm(DN_ALPHA * x_ref[...] + acc_ref[...].T, g_ref[...], b_ref[...])


def _peer(x, wq_bf, sk_bf, u_bf, vt_bf, g, b):
    n = x.shape[0]
    tm = min(ROW_TILE, n)
    n_exp = u_bf.shape[0]
    eb = PEER_EXPERT_BLOCK
    const2 = lambda a: pl.BlockSpec(a.shape, lambda i, k: (0, 0))
    return pl.pallas_call(
        _peer_kernel,
        grid=(n // tm, n_exp // eb),
        in_specs=[pl.BlockSpec((tm, D_MODEL), lambda i, k: (i, 0)),
                  const2(wq_bf),
                  pl.BlockSpec(sk_bf.shape, lambda i, k: (0, 0, 0, 0)),
                  pl.BlockSpec((eb, D_MODEL), lambda i, k: (k, 0)),
                  pl.BlockSpec((D_MODEL, eb), lambda i, k: (0, k)),
                  const2(g), const2(b)],
        out_specs=pl.BlockSpec((tm, D_MODEL), lambda i, k: (i, 0)),
        out_shape=jax.ShapeDtypeStruct((n, D_MODEL), F32),
        scratch_shapes=[pltpu.VMEM((tm, D_MODEL), BF16),
                        pltpu.VMEM((PEER_HEADS, PEER_NKEYS, tm), F32),
                        pltpu.VMEM((PEER_HEADS, PEER_NKEYS, tm), F32),
                        pltpu.VMEM((PEER_HEADS, PEER_NKEYS, tm), BF16),
                        pltpu.VMEM((PEER_HEADS, PEER_NKEYS, tm), BF16),
                        pltpu.VMEM((D_MODEL, tm), F32)],
        compiler_params=_params("parallel", "arbitrary"),
        name="peer_ln3",
    )(x, wq_bf, sk_bf, u_bf, vt_bf, g, b)


def kernel(x_prompt, x_sample, cache_k_cmp, cache_v_cmp, cache_k_slc, cache_v_slc, cache_k_win, cache_v_win,
           state_hgrn, cache_mem_k, cache_mem_v, page_table, mem_prompt, w_in, hgrn_lb_logits, hgrn_norm_g,
           cmp_k_w1, cmp_k_b1, cmp_k_w2, cmp_k_b2, cmp_v_w1, cmp_v_b1, cmp_v_w2, cmp_v_b2, w_out, ln1_g, ln1_b,
           w_mem_q, w_mem_k, w_mem_v, w_mem_o, ln2_g, ln2_b, peer_w_q, peer_sub_keys, peer_u, peer_v,
           ln3_g, ln3_b):
    bsz, seq, _ = x_prompt.shape
    db, dt, _ = x_sample.shape
    dtp = -(-dt // SUBLANES) * SUBLANES
    layer = 0

    w_in_pad = jnp.pad(w_in[layer], ((0, 0), (0, sum(IN_GROUP_WIDTHS) - IN_COLS))).astype(BF16)
    kwts = _cmp_weights(cmp_k_w1[layer], cmp_k_b1[layer], cmp_k_w2[layer], cmp_k_b2[layer])
    vwts = _cmp_weights(cmp_v_w1[layer], cmp_v_b1[layer], cmp_v_w2[layer], cmp_v_b2[layer])
    w_out_bf = w_out[layer].astype(BF16)
    wmq, wmk, wmv, wmo = (w[layer].astype(BF16) for w in (w_mem_q, w_mem_k, w_mem_v, w_mem_o))
    pwq = peer_w_q[layer].astype(BF16)
    psk = peer_sub_keys[layer].astype(BF16)
    pu = peer_u[layer].astype(BF16)
    pvt = peer_v[layer].astype(BF16).T
    vec = lambda a: a[layer].reshape(1, D_MODEL)
    g1, b1, g2, b2, g3, b3 = (vec(a) for a in (ln1_g, ln1_b, ln2_g, ln2_b, ln3_g, ln3_b))
    norm_g = hgrn_norm_g[layer]

    n_p = bsz * seq
    xp = x_prompt.reshape(n_p, D_MODEL)
    zh, nq, kc, vc, ks, vs, kw, vw, ngate = _inproj(xp, w_in_pad)
    per_b = lambda a: a.reshape(bsz, seq, a.shape[-1])
    chunk = min(HG_CHUNK, seq)
    o_h, s_p = _hgrn(per_b(zh), hgrn_lb_logits, norm_g, jnp.zeros((bsz, HG_HEADS, HG_D, HG_D), F32),
                     chunk=chunk, step_tokens=min(HG_STEP_TOKENS, seq), t_valid=chunk)
    kblk, vblk = _cmp_prompt(per_b(kc), per_b(vc), kwts, vwts)
    o_n = _nsa_prompt(per_b(nq), per_b(ngate), kblk, vblk, per_b(ks), per_b(vs), per_b(kw), per_b(vw))
    x1 = _outproj(o_h.reshape(n_p, HG_WIDTH), o_n.reshape(n_p, NSA_WIDTH), xp, w_out_bf, g1, b1)
    n_mem = mem_prompt.shape[1]
    mem_k, mem_v = _memkv(mem_prompt.reshape(bsz * n_mem, D_MODEL), wmk, wmv)
    mem_k = mem_k.reshape(bsz, n_mem, D_MODEL)
    mem_v = mem_v.reshape(bsz, n_mem, D_MODEL)
    x2 = _memattn(x1.reshape(bsz, seq, D_MODEL), mem_k, mem_v, wmq, wmo, g2, b2, nb=1, tm=min(ROW_TILE, seq))
    y_p = _peer(x2.reshape(n_p, D_MODEL), pwq, psk, pu, pvt, g3, b3).reshape(bsz, seq, D_MODEL)

    kv5 = lambda a: a.reshape(1, bsz, seq, NSA_GROUPS, NSA_HD)
    n_win = min(WINDOW, seq)
    win5 = lambda a: a.reshape(bsz, seq, NSA_GROUPS, NSA_HD)[None, :, seq - n_win:]
    mem5 = lambda a: a.reshape(1, bsz, n_mem, MEM_HEADS, MEM_HD)

    n_s = db * dtp
    xs = jnp.pad(x_sample, ((0, 0), (0, dtp - dt), (0, 0))).reshape(n_s, D_MODEL)
    zh, nq, kc_s, vc_s, ks_s, vs_s, kw_s, vw_s, ngate = _inproj(xs, w_in_pad)
    per_s = lambda a: a.reshape(db, dtp, a.shape[-1])
    o_h, s_s = _hgrn(per_s(zh), hgrn_lb_logits, norm_g, state_hgrn[layer],
                     chunk=dtp, step_tokens=dtp, t_valid=dt)
    n_buf = cache_k_win.shape[2]
    o_n = _nsa_sample(page_table, per_s(nq), per_s(ngate), per_s(ks_s), per_s(vs_s), per_s(kw_s), per_s(vw_s),
                      cache_k_win[layer].reshape(db, n_buf, KV_WIDTH), cache_v_win[layer].reshape(db, n_buf, KV_WIDTH),
                      kwts, vwts, cache_k_cmp[layer], cache_v_cmp[layer], cache_k_slc[layer], cache_v_slc[layer],
                      t_valid=dt)
    x1 = _outproj(o_h.reshape(n_s, HG_WIDTH), o_n.reshape(n_s, NSA_WIDTH), xs, w_out_bf, g1, b1)
    nb = min(4, db)
    x2 = _memattn(x1.reshape(db, dtp, D_MODEL), cache_mem_k[layer].reshape(db, n_mem, D_MODEL),
                  cache_mem_v[layer].reshape(db, n_mem, D_MODEL), wmq, wmo, g2, b2, nb=nb, tm=dtp)
    y_s = _peer(x2.reshape(n_s, D_MODEL), pwq, psk, pu, pvt, g3, b3).reshape(db, dtp, D_MODEL)[:, :dt]
    skv5 = lambda a: a.reshape(db, dtp, NSA_GROUPS, NSA_HD)[None, :, :dt]

    return (y_p, y_s, kv5(kc), kv5(vc), kv5(ks), kv5(vs), win5(kw), win5(vw), s_p[None], mem5(mem_k), mem5(mem_v),
            skv5(kc_s), skv5(vc_s), skv5(ks_s), skv5(vs_s), skv5(kw_s), skv5(vw_s), s_s[None])
```

```python
import functools

import numpy as np
import jax
import jax.numpy as jnp
from jax import lax
from jax.experimental import pallas as pl
from jax.experimental.pallas import tpu as pltpu

F32 = jnp.float32
BF16 = jnp.bfloat16

D_MODEL = 1024
HG_HEADS = 4
HG_D = 128
HG_WIDTH = HG_HEADS * HG_D
NSA_HEADS = 8
NSA_GROUPS = 2
NSA_HPG = NSA_HEADS // NSA_GROUPS
NSA_HD = 64
NSA_WIDTH = NSA_HEADS * NSA_HD
KV_WIDTH = NSA_GROUPS * NSA_HD
CMP_BLOCK = 32
CMP_STRIDE = 16
CMP_HIDDEN = 128
SEL_BLOCK = 64
N_SEL = 16
WINDOW = 512
PAGE_SIZE = 128
MEM_HEADS = 4
MEM_HD = D_MODEL // MEM_HEADS
PEER_HEADS = 8
PEER_NKEYS = 128
PEER_KEY_DIM = 128
PEER_TOPK = 16
DN_ALPHA = 2.0 ** 0.25
LN_EPS = 1e-5
NEG_INF = -1e30
LOG2E = 1.4426950408889634
ALIBI_SLOPES = tuple(2.0 ** (-8.0 * (h + 1) / NSA_HEADS) for h in range(NSA_HEADS))

LANES = 128
SUBLANES = 8
VMEM_LIMIT_BYTES = 56 * 1024 * 1024

HG_CHUNK = 32
HG_STEP_TOKENS = 256
NSA_Q_TILE = 128
ROW_TILE = 512
PEER_EXPERT_BLOCK = 1024
SLC_KEY_CHUNK = 512

IN_GROUP_WIDTHS = (4 * HG_WIDTH, NSA_WIDTH) + (KV_WIDTH,) * 6 + (LANES,)
IN_COLS = 4 * HG_WIDTH + NSA_WIDTH + 6 * KV_WIDTH + 3 * NSA_HEADS


def _params(*sem):
    return pltpu.CompilerParams(dimension_semantics=sem, vmem_limit_bytes=VMEM_LIMIT_BYTES)


def _gelu(x):
    return 0.5 * x * (1.0 + jnp.tanh(0.7978845608028654 * (x + 0.044715 * (x * x * x))))


def _layer_norm(y, g, b):
    mu = jnp.mean(y, -1, keepdims=True)
    yc = y - mu
    var = jnp.mean(yc * yc, -1, keepdims=True)
    return yc * lax.rsqrt(var + LN_EPS) * g + b


def _dot(a, b):
    return jnp.dot(a.astype(BF16), b.astype(BF16), preferred_element_type=F32)


def _dot_nt(a, b):
    return lax.dot_general(a.astype(BF16), b.astype(BF16), (((1,), (1,)), ((), ())),
                           preferred_element_type=F32)


def _dot_tn(a, b):
    return lax.dot_general(a.astype(BF16), b.astype(BF16), (((0,), (0,)), ((), ())),
                           preferred_element_type=F32)


def _masked_exp(s, mask):
    sm = jnp.where(mask, s, NEG_INF)
    m = jnp.max(sm, -1, keepdims=True)
    e = jnp.where(mask, jnp.exp(sm - m), 0.0)
    return e, jnp.sum(e, -1, keepdims=True)


def _safe_inv(d):
    return jnp.where(d > 0.0, 1.0 / jnp.where(d > 0.0, d, 1.0), 0.0)


def _inproj_kernel(x_ref, w_ref, *out_refs):
    xb = x_ref[...].astype(BF16)
    off = 0
    for o_ref in out_refs:
        width = o_ref.shape[-1]
        o_ref[...] = jnp.dot(xb, w_ref[:, off:off + width], preferred_element_type=F32)
        off += width


def _inproj(x, w_pad):
    n = x.shape[0]
    tm = min(ROW_TILE, n)
    return pl.pallas_call(
        _inproj_kernel,
        grid=(n // tm,),
        in_specs=[pl.BlockSpec((tm, D_MODEL), lambda i: (i, 0)),
                  pl.BlockSpec(w_pad.shape, lambda i: (0, 0))],
        out_specs=[pl.BlockSpec((tm, w), lambda i: (i, 0)) for w in IN_GROUP_WIDTHS],
        out_shape=[jax.ShapeDtypeStruct((n, w), F32) for w in IN_GROUP_WIDTHS],
        compiler_params=_params("parallel"),
        name="inproj",
    )(x, w_pad)


def _hgrn_kernel(zh_ref, lbl_ref, ng_ref, s0_ref, tri_ref, o_ref, sout_ref, st_ref, obuf_ref,
                 *, chunk, t_valid):
    ti = pl.program_id(1)
    step_tokens = zh_ref.shape[1]

    @pl.when(ti == 0)
    def _():
        for h in range(HG_HEADS):
            st_ref[h] = s0_ref[0, h].T

    logits = lbl_ref[...]
    le = jnp.exp(logits - jnp.max(logits, axis=0, keepdims=True))
    lb_all = le[0:1] / jnp.sum(le, axis=0, keepdims=True)
    norm_g = ng_ref[...]
    row = lax.broadcasted_iota(jnp.int32, (chunk, 1), 0)

    def do_chunk(c, carry):
        r0 = pl.multiple_of(c * chunk, chunk)
        rows = pl.ds(r0, chunk)
        for h in range(HG_HEADS):
            cs = h * HG_D
            q = zh_ref[0, rows, cs:cs + HG_D]
            zf = zh_ref[0, rows, HG_WIDTH + cs:HG_WIDTH + cs + HG_D]
            v = zh_ref[0, rows, 2 * HG_WIDTH + cs:2 * HG_WIDTH + cs + HG_D]
            zg = zh_ref[0, rows, 3 * HG_WIDTH + cs:3 * HG_WIDTH + cs + HG_D]
            lb = lb_all[:, cs:cs + HG_D]
            lf = jnp.log(lb + (1.0 - lb) * jax.nn.sigmoid(zf))
            k = (1.0 - lb) * jax.nn.sigmoid(-zf)
            if t_valid < chunk:
                lf = jnp.where(row < t_valid, lf, 0.0)
                k = jnp.where(row < t_valid, k, 0.0)
            if chunk <= SUBLANES:
                g = jnp.where(row >= 0, lf[0:1], 0.0)
                for t in range(1, chunk):
                    g = g + jnp.where(row >= t, lf[t:t + 1], 0.0)
            else:
                g = jnp.dot(tri_ref[...], lf, precision=lax.Precision.HIGHEST, preferred_element_type=F32)
            st = st_ref[h]
            o_inter = _dot_nt(q * jnp.exp(g), st)
            for t in range(chunk):
                r8 = -(-(t + 1) // SUBLANES) * SUBLANES
                rel = jnp.where(row[:r8] <= t, g[t:t + 1] - g[:r8], NEG_INF)
                p = (q[t:t + 1] * k[:r8]) * jnp.exp(rel)
                w = jnp.sum(p, axis=-1, keepdims=True)
                obuf_ref[t:t + 1, :] = jnp.sum(w * v[:r8], axis=0, keepdims=True)
            o = o_inter + obuf_ref[...]
            g_end = g[chunk - 1:chunk]
            kk = k * jnp.exp(g_end - g)
            st_ref[h] = st * jnp.exp(g_end) + _dot_tn(v, kk)
            o = o * lax.rsqrt(jnp.mean(o * o, -1, keepdims=True) + LN_EPS) * norm_g
            o_ref[0, rows, cs:cs + HG_D] = o * jax.nn.sigmoid(zg)
        return carry

    lax.fori_loop(0, step_tokens // chunk, do_chunk, 0)

    @pl.when(ti == pl.num_programs(1) - 1)
    def _():
        for h in range(HG_HEADS):
            sout_ref[0, h] = st_ref[h].T


def _hgrn(zh, lb_logits, norm_g, s0, *, chunk, step_tokens, t_valid):
    b, t, _ = zh.shape
    tri = jnp.asarray(np.tril(np.ones((chunk, chunk), np.float32)))
    kern = functools.partial(_hgrn_kernel, chunk=chunk, t_valid=t_valid)
    return pl.pallas_call(
        kern,
        grid=(b, t // step_tokens),
        in_specs=[pl.BlockSpec((1, step_tokens, 4 * HG_WIDTH), lambda i, j: (i, j, 0)),
                  pl.BlockSpec(lb_logits.shape, lambda i, j: (0, 0)),
                  pl.BlockSpec((1, HG_D), lambda i, j: (0, 0)),
                  pl.BlockSpec((1, HG_HEADS, HG_D, HG_D), lambda i, j: (i, 0, 0, 0)),
                  pl.BlockSpec((chunk, chunk), lambda i, j: (0, 0))],
        out_specs=[pl.BlockSpec((1, step_tokens, HG_WIDTH), lambda i, j: (i, j, 0)),
                   pl.BlockSpec((1, HG_HEADS, HG_D, HG_D), lambda i, j: (i, 0, 0, 0))],
        out_shape=[jax.ShapeDtypeStruct((b, t, HG_WIDTH), F32),
                   jax.ShapeDtypeStruct((b, HG_HEADS, HG_D, HG_D), F32)],
        scratch_shapes=[pltpu.VMEM((HG_HEADS, HG_D, HG_D), F32), pltpu.VMEM((chunk, HG_D), F32)],
        compiler_params=_params("parallel", "arbitrary"),
        name="hgrn",
    )(zh, lb_logits, norm_g.reshape(1, HG_D), s0, tri)


def _cmp_blocks(view, w1_ref, b1_ref, w2_ref, b2_ref):
    return _cmp_finish(_dot(view, w1_ref[...]), b1_ref, w2_ref, b2_ref)


def _cmp_blocks_from_rows(rows_ref, w1_ref, b1_ref, w2_ref, b2_ref):
    n_sub = rows_ref.shape[0] // CMP_STRIDE
    a = None
    for p in range(CMP_STRIDE):
        part = _dot(rows_ref[pl.ds(p, n_sub, stride=CMP_STRIDE), :], w1_ref[p * KV_WIDTH:(p + 1) * KV_WIDTH, :])
        a = part if a is None else a + part
    return _cmp_finish(a, b1_ref, w2_ref, b2_ref)


def _cmp_finish(a, b1_ref, w2_ref, b2_ref):
    n_sub = a.shape[0]
    hid = []
    for g in range(NSA_GROUPS):
        a0 = a[:, (2 * g) * CMP_HIDDEN:(2 * g + 1) * CMP_HIDDEN]
        a1 = a[:, (2 * g + 1) * CMP_HIDDEN:(2 * g + 2) * CMP_HIDDEN]
        hid.append(a0 + pltpu.roll(a1, n_sub - 1, axis=0))
    h = jnp.concatenate(hid, axis=-1) + b1_ref[...]
    return _dot(_gelu(h), w2_ref[...]) + b2_ref[...]


def _cmp_kernel(kv_ref, vv_ref, kw1, kb1, kw2, kb2, vw1, vb1, vw2, vb2, ko_ref, vo_ref):
    ko_ref[0] = _cmp_blocks(kv_ref[0], kw1, kb1, kw2, kb2)
    vo_ref[0] = _cmp_blocks(vv_ref[0], vw1, vb1, vw2, vb2)


def _cmp_weights(w1, b1, w2, b2):
    w1r = w1.reshape(CMP_BLOCK // CMP_STRIDE, CMP_STRIDE, NSA_HD, CMP_HIDDEN)
    big = jnp.zeros((CMP_STRIDE, NSA_GROUPS, NSA_HD, NSA_GROUPS, 2, CMP_HIDDEN), F32)
    for g in range(NSA_GROUPS):
        big = big.at[:, g, :, g, :, :].set(jnp.transpose(w1r, (1, 2, 0, 3)))
    w1b = big.reshape(CMP_STRIDE * KV_WIDTH, NSA_GROUPS * 2 * CMP_HIDDEN).astype(BF16)
    w2b = jnp.zeros((NSA_GROUPS, CMP_HIDDEN, NSA_GROUPS, NSA_HD), F32)
    for g in range(NSA_GROUPS):
        w2b = w2b.at[g, :, g, :].set(w2)
    w2b = w2b.reshape(NSA_GROUPS * CMP_HIDDEN, KV_WIDTH).astype(BF16)
    b1b = jnp.tile(b1, NSA_GROUPS).reshape(1, NSA_GROUPS * CMP_HIDDEN)
    b2b = jnp.tile(b2, NSA_GROUPS).reshape(1, KV_WIDTH)
    return w1b, b1b, w2b, b2b


def _cmp_prompt(kc, vc, kwts, vwts):
    b, t, _ = kc.shape
    n_sub = t // CMP_STRIDE
    view = lambda a: a.reshape(b, n_sub, CMP_STRIDE * KV_WIDTH)
    wspecs = [pl.BlockSpec(w.shape, lambda i: (0, 0)) for w in kwts + vwts]
    return pl.pallas_call(
        _cmp_kernel,
        grid=(b,),
        in_specs=[pl.BlockSpec((1, n_sub, CMP_STRIDE * KV_WIDTH), lambda i: (i, 0, 0))] * 2 + wspecs,
        out_specs=[pl.BlockSpec((1, n_sub, KV_WIDTH), lambda i: (i, 0, 0))] * 2,
        out_shape=[jax.ShapeDtypeStruct((b, n_sub, KV_WIDTH), F32)] * 2,
        compiler_params=_params("parallel"),
        name="cmp_prompt",
    )(view(kc), view(vc), *kwts, *vwts)


def _cmp_to_slc(n_cmp, n_slc):
    cs = np.arange(n_cmp)[:, None] * CMP_STRIDE
    ss = np.arange(n_slc)[None, :] * SEL_BLOCK
    shared = np.clip(np.minimum(cs + CMP_BLOCK, ss + SEL_BLOCK) - np.maximum(cs, ss), 0, None)
    return (shared / CMP_STRIDE).astype(np.float32)


def _softmax2(s):
    m = jnp.max(s, -1, keepdims=True)
    e = jnp.exp2(s - m)
    inv = jnp.where(m > 0.5 * NEG_INF, 1.0 / jnp.sum(e, -1, keepdims=True), 0.0)
    return e, inv


def _nsa_prompt_kernel(q_ref, gate_ref, kb_ref, vb_ref, ks_ref, vs_ref, kw_ref, vw_ref, mt_ref, ex_ref,
                       o_ref, bias_ref, *, n_cmp, n_var):
    tq = q_ref.shape[1]
    t_all = ks_ref.shape[1]
    n_sub = kb_ref.shape[1]
    n_slc = mt_ref.shape[0]
    tile = pl.program_id(1)
    t0 = tile * tq
    q = q_ref[0] * (NSA_HD ** -0.5 * LOG2E)
    slopes = tuple(s * LOG2E for s in ALIBI_SLOPES)
    gates = jax.nn.sigmoid(gate_ref[0])
    qpos = t0 + lax.broadcasted_iota(jnp.int32, (tq, 1), 0)

    cidx = lax.broadcasted_iota(jnp.int32, (1, n_sub), 1)
    kend = cidx * CMP_STRIDE + CMP_BLOCK - 1
    bias_c = jnp.where((qpos - kend >= 0) & (cidx < n_cmp), 0.0, NEG_INF)
    kend_rel = (kend - t0).astype(F32)
    nidx = lax.broadcasted_iota(jnp.int32, (n_slc, 1), 0)
    tpos = t0 + lax.broadcasted_iota(jnp.int32, (1, tq), 1)
    elig = nidx * SEL_BLOCK <= tpos
    force = (nidx == tpos // SEL_BLOCK) | (nidx == 0)
    causal = qpos - lax.broadcasted_iota(jnp.int32, (1, t_all), 1) >= 0
    span = min(WINDOW + tq, t_all)
    start = pl.multiple_of(jnp.clip(t0 - WINDOW, 0, t_all - span), tq)
    kpos_w = start + lax.broadcasted_iota(jnp.int32, (1, span), 1)
    bias_w = jnp.where((qpos - kpos_w >= 0) & (qpos - kpos_w < WINDOW), 0.0, NEG_INF)
    kpos_w_rel = (kpos_w - t0).astype(F32)

    for g in range(NSA_GROUPS):
        gs = g * NSA_HD
        kb = kb_ref[0, :, gs:gs + NSA_HD]
        vb = vb_ref[0, :, gs:gs + NSA_HD]
        kw = kw_ref[0, pl.ds(start, span), gs:gs + NSA_HD]
        vw = vw_ref[0, pl.ds(start, span), gs:gs + NSA_HD]
        psum = jnp.zeros((tq, n_sub), F32)
        for j in range(NSA_HPG):
            h = g * NSA_HPG + j
            qh = q[:, h * NSA_HD:(h + 1) * NSA_HD]
            e, inv = _softmax2(_dot_nt(qh, kb) + bias_c + slopes[h] * kend_rel)
            p = e * inv
            psum = psum + p
            o_cmp = _dot(p, vb)
            e, inv = _softmax2(_dot_nt(qh, kw) + bias_w + slopes[h] * kpos_w_rel)
            o_win = _dot(e, vw) * inv
            o_ref[0, :, h * NSA_HD:(h + 1) * NSA_HD] = (
                gates[:, 3 * h:3 * h + 1] * o_cmp + gates[:, 3 * h + 2:3 * h + 3] * o_win)
        imp_t = lax.dot_general(mt_ref[...], psum, (((1,), (1,)), ((), ())),
                                precision=lax.Precision.HIGHEST, preferred_element_type=F32)
        val = jnp.where(force, jnp.inf, jnp.where(elig, imp_t, -jnp.inf))
        cnt = jnp.zeros((n_slc, tq), F32)
        for m in range(n_slc):
            other = val[m:m + 1]
            tie = jnp.where(nidx > m, 1.0, 0.0)
            cnt = cnt + jnp.where(other > val, 1.0, jnp.where(other == val, tie, 0.0))
        sel_t = jnp.where(cnt < float(min(N_SEL, n_slc)), 1.0, 0.0)
        sel_keys = _dot(sel_t.T, ex_ref[...])
        bias_ref[g] = jnp.where((sel_keys > 0.5) & causal, 0.0, NEG_INF)

    tiles_per_var = (t_all // tq) // n_var
    for v in range(n_var):
        n_keys = (v + 1) * tiles_per_var * tq

        @pl.when(tile // tiles_per_var == v)
        def _(n_keys=n_keys):
            kpos_rel = (lax.broadcasted_iota(jnp.int32, (1, n_keys), 1) - t0).astype(F32)
            for g in range(NSA_GROUPS):
                gs = g * NSA_HD
                ks = ks_ref[0, 0:n_keys, gs:gs + NSA_HD]
                vs = vs_ref[0, 0:n_keys, gs:gs + NSA_HD]
                bias_s = bias_ref[g, :, 0:n_keys]
                for j in range(NSA_HPG):
                    h = g * NSA_HPG + j
                    qh = q[:, h * NSA_HD:(h + 1) * NSA_HD]
                    e, inv = _softmax2(_dot_nt(qh, ks) + bias_s + slopes[h] * kpos_rel)
                    o_ref[0, :, h * NSA_HD:(h + 1) * NSA_HD] += gates[:, 3 * h + 1:3 * h + 2] * (_dot(e, vs) * inv)


def _nsa_prompt(nq, ng, kblk, vblk, ks, vs, kw, vw):
    b, t, _ = nq.shape
    tq = min(NSA_Q_TILE, t)
    n_sub = kblk.shape[1]
    n_cmp = n_sub - CMP_BLOCK // CMP_STRIDE + 1
    n_slc = -(-t // SEL_BLOCK)
    n_var = max(1, t // SLC_KEY_CHUNK)
    assert (t // tq) % n_var == 0
    mt = np.zeros((n_slc, n_sub), np.float32)
    mt[:, :n_cmp] = _cmp_to_slc(n_cmp, n_slc).T
    expand = (np.arange(t)[None, :] // SEL_BLOCK == np.arange(n_slc)[:, None]).astype(np.float32)
    full = lambda w: pl.BlockSpec((1, t, w), lambda i, j: (i, 0, 0))
    return pl.pallas_call(
        functools.partial(_nsa_prompt_kernel, n_cmp=n_cmp, n_var=n_var),
        grid=(b, t // tq),
        in_specs=[pl.BlockSpec((1, tq, NSA_WIDTH), lambda i, j: (i, j, 0)),
                  pl.BlockSpec((1, tq, LANES), lambda i, j: (i, j, 0)),
                  pl.BlockSpec((1, n_sub, KV_WIDTH), lambda i, j: (i, 0, 0)),
                  pl.BlockSpec((1, n_sub, KV_WIDTH), lambda i, j: (i, 0, 0)),
                  full(KV_WIDTH), full(KV_WIDTH), full(KV_WIDTH), full(KV_WIDTH),
                  pl.BlockSpec(mt.shape, lambda i, j: (0, 0)),
                  pl.BlockSpec(expand.shape, lambda i, j: (0, 0))],
        out_specs=pl.BlockSpec((1, tq, NSA_WIDTH), lambda i, j: (i, j, 0)),
        out_shape=jax.ShapeDtypeStruct((b, t, NSA_WIDTH), F32),
        scratch_shapes=[pltpu.VMEM((NSA_GROUPS, tq, t), F32)],
        compiler_params=_params("parallel", "arbitrary"),
        name="nsa_prompt",
    )(nq, ng, kblk, vblk, ks, vs, kw, vw, jnp.asarray(mt), jnp.asarray(expand, dtype=BF16))


def _nsa_sample_kernel(pt_ref, q_ref, gate_ref, ksn_ref, vsn_ref, kwn_ref, vwn_ref, wk_ref, wv_ref,
                       kw1, kb1, kw2, kb2, vw1, vb1, vw2, vb2, m_ref, ex_ref,
                       kc_pool, vc_pool, ks_pool, vs_pool,
                       o_ref, kc_buf, vc_buf, ks_buf, vs_buf, sem, *, t_valid, n_slc):
    b = pl.program_id(0)
    n_pages = pt_ref.shape[1]
    tp = q_ref.shape[1]
    past = n_pages * PAGE_SIZE
    n_sub = past // CMP_STRIDE
    n_cmp = n_sub - CMP_BLOCK // CMP_STRIDE + 1
    n_past_blocks = past // SEL_BLOCK

    pools = ((kc_pool, kc_buf, False), (vc_pool, vc_buf, False), (ks_pool, ks_buf, True), (vs_pool, vs_buf, True))

    def page_copy(i, j):
        pool, buf, rows_on_lanes = pools[i]
        rows = pl.ds(pl.multiple_of(j * PAGE_SIZE, PAGE_SIZE), PAGE_SIZE)
        dst = buf.at[:, :, rows] if rows_on_lanes else buf.at[rows]
        return pltpu.make_async_copy(pool.at[pt_ref[b, j]], dst, sem.at[i])

    def start_pages(which):
        def body(j, carry):
            for i in which:
                page_copy(i, j).start()
            return carry
        lax.fori_loop(0, n_pages, body, 0)

    def wait_pages(which):
        def body(j, carry):
            for i in which:
                page_copy(i, j).wait()
            return carry
        lax.fori_loop(0, n_pages, body, 0)

    start_pages((0, 1))
    start_pages((2, 3))
    wait_pages((0, 1))
    kblk = _cmp_blocks_from_rows(kc_buf, kw1, kb1, kw2, kb2)
    vblk = _cmp_blocks_from_rows(vc_buf, vw1, vb1, vw2, vb2)
    wait_pages((2, 3))

    q = q_ref[0] * (NSA_HD ** -0.5)
    gates = jax.nn.sigmoid(gate_ref[0])
    trow = lax.broadcasted_iota(jnp.int32, (tp, 1), 0)
    trow_q = jnp.concatenate([trow] * NSA_HPG, axis=0)
    qpos_q = past + trow_q
    cidx = lax.broadcasted_iota(jnp.int32, (1, n_sub), 1)
    dist_c = qpos_q - (cidx * CMP_STRIDE + CMP_BLOCK - 1)
    mask_c = (dist_c >= 0) & (cidx < n_cmp)
    dist_cf = dist_c.astype(F32)
    n_lane = m_ref.shape[1]
    nidx = lax.broadcasted_iota(jnp.int32, (tp, n_lane), 1)
    qpos_t = past + trow
    elig = (nidx * SEL_BLOCK <= qpos_t) & (nidx < n_slc)
    force = ((nidx == qpos_t // SEL_BLOCK) | (nidx == 0)) & (nidx < n_slc)
    kpos = lax.broadcasted_iota(jnp.int32, (1, past), 1)
    dist_pf = (qpos_q - kpos).astype(F32)
    rnew = lax.broadcasted_iota(jnp.int32, (1, tp), 1)
    dist_n = trow_q - rnew
    mask_n = (dist_n >= 0) & (rnew < t_valid)
    dist_nf = dist_n.astype(F32)
    n_buf = wk_ref.shape[3]
    dist_b = qpos_q - (past - n_buf + lax.broadcasted_iota(jnp.int32, (1, n_buf), 1))
    mask_b = (dist_b >= 0) & (dist_b < WINDOW)
    dist_bf = dist_b.astype(F32)

    for g in range(NSA_GROUPS):
        gs = g * NSA_HD
        heads = range(g * NSA_HPG, (g + 1) * NSA_HPG)
        qg = jnp.concatenate([q[:, h * NSA_HD:(h + 1) * NSA_HD] for h in heads], axis=0)
        slope = jnp.concatenate([jnp.full((tp, 1), ALIBI_SLOPES[h], F32) for h in heads], axis=0)
        s = _dot_nt(qg, kblk[:, gs:gs + NSA_HD]) - slope * dist_cf
        e, d = _masked_exp(s, mask_c)
        p = e * _safe_inv(d)
        o_cmp = _dot(p, vblk[:, gs:gs + NSA_HD])
        psum = p[0:tp]
        for j in range(1, NSA_HPG):
            psum = psum + p[j * tp:(j + 1) * tp]
        imp = jnp.dot(psum, m_ref[...], precision=lax.Precision.HIGHEST, preferred_element_type=F32)
        val = jnp.where(force, jnp.inf, jnp.where(elig, imp, -jnp.inf))
        sel = jnp.zeros((tp, n_lane), F32)
        for _ in range(min(N_SEL, n_slc)):
            m = jnp.max(val, axis=-1, keepdims=True)
            first = jnp.min(jnp.where(val == m, nidx, n_lane), axis=-1, keepdims=True)
            pick = nidx == first
            sel = jnp.where(pick, 1.0, sel)
            val = jnp.where(pick, -jnp.inf, val)
        sel = jnp.where(nidx < n_slc, sel, 0.0)
        sel_keys = _dot(sel[:, :n_past_blocks], ex_ref[...])
        mask_p = jnp.concatenate([sel_keys] * NSA_HPG, axis=0) > 0.5
        sel_new = jnp.concatenate([sel[:, n_past_blocks:n_past_blocks + 1]] * NSA_HPG, axis=0) > 0.5
        s_p = _dot(qg, ks_buf[g]) - slope * dist_pf
        s_n = _dot_nt(qg, ksn_ref[0, :, gs:gs + NSA_HD]) - slope * dist_nf
        mask_sn = mask_n & sel_new
        mx = jnp.maximum(jnp.max(jnp.where(mask_p, s_p, NEG_INF), -1, keepdims=True),
                         jnp.max(jnp.where(mask_sn, s_n, NEG_INF), -1, keepdims=True))
        e_p = jnp.where(mask_p, jnp.exp(jnp.where(mask_p, s_p, NEG_INF) - mx), 0.0)
        e_n = jnp.where(mask_sn, jnp.exp(jnp.where(mask_sn, s_n, NEG_INF) - mx), 0.0)
        den = jnp.sum(e_p, -1, keepdims=True) + jnp.sum(e_n, -1, keepdims=True)
        o_slc = (_dot_nt(e_p, vs_buf[g]) + _dot(e_n, vsn_ref[0, :, gs:gs + NSA_HD])) * _safe_inv(den)
        s_b = _dot(qg, wk_ref[0, g]) - slope * dist_bf
        s_n = _dot_nt(qg, kwn_ref[0, :, gs:gs + NSA_HD]) - slope * dist_nf
        mx = jnp.maximum(jnp.max(jnp.where(mask_b, s_b, NEG_INF), -1, keepdims=True),
                         jnp.max(jnp.where(mask_n, s_n, NEG_INF), -1, keepdims=True))
        e_b = jnp.where(mask_b, jnp.exp(jnp.where(mask_b, s_b, NEG_INF) - mx), 0.0)
        e_n = jnp.where(mask_n, jnp.exp(jnp.where(mask_n, s_n, NEG_INF) - mx), 0.0)
        den = jnp.sum(e_b, -1, keepdims=True) + jnp.sum(e_n, -1, keepdims=True)
        o_win = (_dot_nt(e_b, wv_ref[0, g]) + _dot(e_n, vwn_ref[0, :, gs:gs + NSA_HD])) * _safe_inv(den)
        for j, h in enumerate(heads):
            rs = slice(j * tp, (j + 1) * tp)
            o_ref[0, :, h * NSA_HD:(h + 1) * NSA_HD] = (
                gates[:, 3 * h:3 * h + 1] * o_cmp[rs]
                + gates[:, 3 * h + 1:3 * h + 2] * o_slc[rs]
                + gates[:, 3 * h + 2:3 * h + 3] * o_win[rs])


def _nsa_sample(page_table, nq, ng, ksn, vsn, kwn, vwn, win_k, win_v, kwts, vwts,
                pool_kc, pool_vc, pool_ks, pool_vs, *, t_valid):
    db, tp, _ = nq.shape
    n_pages = page_table.shape[1]
    past = n_pages * PAGE_SIZE
    n_phys = pool_kc.shape[0]
    n_sub = past // CMP_STRIDE
    n_cmp = n_sub - CMP_BLOCK // CMP_STRIDE + 1
    n_slc = -(-(past + t_valid) // SEL_BLOCK)
    n_lane = -(-n_slc // LANES) * LANES
    n_buf = win_k.shape[3]
    m = np.zeros((n_sub, n_lane), np.float32)
    m[:n_cmp, :n_slc] = _cmp_to_slc(n_cmp, n_slc)
    n_past_blocks = past // SEL_BLOCK
    expand = (np.arange(past)[None, :] // SEL_BLOCK == np.arange(n_past_blocks)[:, None]).astype(np.float32)
    row_view = lambda p: p.reshape(n_phys, PAGE_SIZE, KV_WIDTH)
    col_view = lambda p: jnp.transpose(p, (0, 2, 3, 1))
    per_b = lambda r, w: pl.BlockSpec((1, r, w), lambda i, pt: (i, 0, 0))
    const = lambda a: pl.BlockSpec(a.shape, lambda i, pt: (0,) * a.ndim)
    wts = kwts + vwts
    m_j = jnp.asarray(m)
    ex_j = jnp.asarray(expand, dtype=BF16)
    any_spec = pl.BlockSpec(memory_space=pl.ANY)
    grid_spec = pltpu.PrefetchScalarGridSpec(
        num_scalar_prefetch=1,
        grid=(db,),
        in_specs=[per_b(tp, NSA_WIDTH), per_b(tp, LANES)] + [per_b(tp, KV_WIDTH)] * 4
                 + [pl.BlockSpec((1, NSA_GROUPS, NSA_HD, n_buf), lambda i, pt: (i, 0, 0, 0))] * 2
                 + [const(w) for w in wts] + [const(m_j), const(ex_j)]
                 + [any_spec] * 4,
        out_specs=per_b(tp, NSA_WIDTH),
        scratch_shapes=[pltpu.VMEM((past, KV_WIDTH), F32)] * 2
                       + [pltpu.VMEM((NSA_GROUPS, NSA_HD, past), F32)] * 2
                       + [pltpu.SemaphoreType.DMA((4,))],
    )
    return pl.pallas_call(
        functools.partial(_nsa_sample_kernel, t_valid=t_valid, n_slc=n_slc),
        grid_spec=grid_spec,
        out_shape=jax.ShapeDtypeStruct((db, tp, NSA_WIDTH), F32),
        compiler_params=_params("arbitrary"),
        name="nsa_sample",
    )(page_table, nq, ng, ksn, vsn, kwn, vwn, win_k, win_v, *wts, m_j, ex_j,
      row_view(pool_kc), row_view(pool_vc), col_view(pool_ks), col_view(pool_vs))


def _outproj_kernel(oh_ref, on_ref, x_ref, w_ref, g_ref, b_ref, y_ref):
    mix = (jnp.dot(oh_ref[...].astype(BF16), w_ref[:HG_WIDTH, :], preferred_element_type=F32)
           + jnp.dot(on_ref[...].astype(BF16), w_ref[HG_WIDTH:, :], preferred_element_type=F32))
    y_ref[...] = _layer_norm(DN_ALPHA * x_ref[...] + mix, g_ref[...], b_ref[...])


def _outproj(o_h, o_n, x, w_bf, g, b):
    n = x.shape[0]
    tm = min(ROW_TILE, n)
    row = lambda w: pl.BlockSpec((tm, w), lambda i: (i, 0))
    const = lambda a: pl.BlockSpec(a.shape, lambda i: (0, 0))
    return pl.pallas_call(
        _outproj_kernel,
        grid=(n // tm,),
        in_specs=[row(HG_WIDTH), row(NSA_WIDTH), row(D_MODEL), const(w_bf), const(g), const(b)],
        out_specs=row(D_MODEL),
        out_shape=jax.ShapeDtypeStruct((n, D_MODEL), F32),
        compiler_params=_params("parallel"),
        name="outproj_ln1",
    )(o_h, o_n, x, w_bf, g, b)


def _memkv_kernel(m_ref, wk_ref, wv_ref, k_ref, v_ref):
    mb = m_ref[...].astype(BF16)
    k_ref[...] = jnp.dot(mb, wk_ref[...], preferred_element_type=F32)
    v_ref[...] = jnp.dot(mb, wv_ref[...], preferred_element_type=F32)


def _memkv(mem, wk_bf, wv_bf):
    n = mem.shape[0]
    tm = min(ROW_TILE, n)
    row = pl.BlockSpec((tm, D_MODEL), lambda i: (i, 0))
    const = lambda a: pl.BlockSpec(a.shape, lambda i: (0, 0))
    return pl.pallas_call(
        _memkv_kernel,
        grid=(n // tm,),
        in_specs=[row, const(wk_bf), const(wv_bf)],
        out_specs=[row, row],
        out_shape=[jax.ShapeDtypeStruct((n, D_MODEL), F32)] * 2,
        compiler_params=_params("parallel"),
        name="mem_kv",
    )(mem, wk_bf, wv_bf)


def _memattn_kernel(x_ref, mk_ref, mv_ref, wq_ref, wo_ref, g_ref, b_ref, y_ref, o_buf):
    nb, tm, _ = x_ref.shape
    x = x_ref[...].reshape(nb * tm, D_MODEL)
    q = jnp.dot(x.astype(BF16), wq_ref[...], preferred_element_type=F32) * (MEM_HD ** -0.5)
    for i in range(nb):
        rs = slice(i * tm, (i + 1) * tm)
        for h in range(MEM_HEADS):
            cs = slice(h * MEM_HD, (h + 1) * MEM_HD)
            s = _dot_nt(q[rs, cs], mk_ref[i, :, cs])
            e = jnp.exp(s - jnp.max(s, -1, keepdims=True))
            o_buf[rs, cs] = _dot(e, mv_ref[i, :, cs]) / jnp.sum(e, -1, keepdims=True)
    att = jnp.dot(o_buf[...].astype(BF16), wo_ref[...], preferred_element_type=F32)
    y = _layer_norm(DN_ALPHA * x + att, g_ref[...], b_ref[...])
    y_ref[...] = y.reshape(nb, tm, D_MODEL)


def _memattn(x, mem_k, mem_v, wq_bf, wo_bf, g, b, *, nb, tm):
    bsz, t, _ = x.shape
    n_mem = mem_k.shape[1]
    const = lambda a: pl.BlockSpec(a.shape, lambda i, j: (0, 0))
    return pl.pallas_call(
        _memattn_kernel,
        grid=(bsz // nb, t // tm),
        in_specs=[pl.BlockSpec((nb, tm, D_MODEL), lambda i, j: (i, j, 0)),
                  pl.BlockSpec((nb, n_mem, D_MODEL), lambda i, j: (i, 0, 0)),
                  pl.BlockSpec((nb, n_mem, D_MODEL), lambda i, j: (i, 0, 0)),
                  const(wq_bf), const(wo_bf), const(g), const(b)],
        out_specs=pl.BlockSpec((nb, tm, D_MODEL), lambda i, j: (i, j, 0)),
        out_shape=jax.ShapeDtypeStruct((bsz, t, D_MODEL), F32),
        scratch_shapes=[pltpu.VMEM((nb * tm, D_MODEL), F32)],
        compiler_params=_params("parallel", "arbitrary"),
        name="mem_attn_ln2",
    )(x, mem_k, mem_v, wq_bf, wo_bf, g, b)


def _take_top(s, idx, n_take, tie_safe):
    rank = jnp.full(s.shape, float(n_take), F32)
    tops = []
    for i in range(n_take):
        m = jnp.max(s, axis=0, keepdims=True)
        hit = s == m
        if tie_safe:
            first = jnp.min(jnp.where(hit, idx, 1e9), axis=0, keepdims=True)
            hit = idx == first
        rank = jnp.where(hit, float(i), rank)
        s = jnp.where(hit, -jnp.inf, s)
        tops.append(m)
    return rank, tops, s


def _n_removed(s):
    return jnp.sum(jnp.where(s == -jnp.inf, 1.0, 0.0), axis=0, keepdims=True)


def _peer_route(h, lanes, qh_ref, sk_ref, lr_ref, e1_ref, rank2_ref, e2_ref, tie_safe):
    k = PEER_TOPK
    s1 = _dot_nt(sk_ref[h, 0], qh_ref[h, lanes, 0:PEER_KEY_DIM])
    s2 = _dot_nt(sk_ref[h, 1], qh_ref[h, lanes, PEER_KEY_DIM:2 * PEER_KEY_DIM])
    kidx = lax.broadcasted_iota(jnp.int32, (PEER_NKEYS, 1), 0).astype(F32)
    rank1, top1, rest1 = _take_top(s1, kidx, k, tie_safe)
    rank2, top2, rest2 = _take_top(s2, kidx, k, tie_safe)
    t2 = jnp.concatenate(top2, axis=0)
    r8 = lax.broadcasted_iota(jnp.int32, (SUBLANES, 1), 0).astype(F32)
    r16 = lax.broadcasted_iota(jnp.int32, (2 * SUBLANES, 1), 0).astype(F32)
    blocks = [top1[0] + t2]
    flat = [r16]
    off = [jnp.zeros_like(r16)]
    for i in range(1, SUBLANES):
        blocks.append(top1[i] + t2[:SUBLANES])
        flat.append(float(i * k) + r8)
        off.append(jnp.where(r8 < float(k // (i + 1)), 0.0, -jnp.inf))
    blocks.append(jnp.concatenate(top1[SUBLANES:], axis=0) + top2[0])
    flat.append(float(k) * (r8 + float(SUBLANES)))
    off.append(jnp.zeros_like(r8))
    cand = jnp.concatenate(blocks, axis=0) + jnp.concatenate(off, axis=0)
    _, ctops, rest_c = _take_top(cand, jnp.concatenate(flat, axis=0), k, tie_safe)
    cnt = jnp.where(rest_c == -jnp.inf, 1.0, 0.0) - jnp.where(cand == -jnp.inf, 1.0, 0.0)
    z = jnp.sum(cnt * jnp.exp(cand - ctops[0]), axis=0, keepdims=True)
    lrow = jnp.zeros(s1.shape, F32)
    row0 = 0
    for i in range(SUBLANES):
        n_rows = 2 * SUBLANES if i == 0 else SUBLANES
        l_i = jnp.sum(cnt[row0:row0 + n_rows], axis=0, keepdims=True)
        lrow = jnp.where(rank1 == float(i), l_i, lrow)
        row0 += n_rows
    for r in range(SUBLANES):
        lrow = jnp.where(rank1 == float(SUBLANES + r), cnt[row0 + r:row0 + r + 1], lrow)
    lr_ref[h, :, lanes] = lrow
    e1_ref[h, :, lanes] = jnp.exp(s1 - top1[0]) * (1.0 / z)
    rank2_ref[h, :, lanes] = rank2.astype(BF16)
    e2_ref[h, :, lanes] = jnp.exp(s2 - top2[0]).astype(BF16)
    bad = (jnp.abs(_n_removed(rest1) - float(k)) + jnp.abs(_n_removed(rest2) - float(k))
           + jnp.abs(jnp.sum(cnt, axis=0, keepdims=True) - float(k)))
    return jnp.max(bad) > 0.5


def _peer_kernel(x_ref, wq_ref, sk_ref, u_ref, vt_ref, g_ref, b_ref, y_ref,
                 xt_ref, qh_ref, lr_ref, e1_ref, rank2_ref, e2_ref, acc_ref):
    kb = pl.program_id(1)
    tm = x_ref.shape[0]
    eb = u_ref.shape[0]
    a_per_block = eb // PEER_NKEYS

    @pl.when(kb == 0)
    def _():
        x = x_ref[...]
        xt_ref[...] = x.T.astype(BF16)
        acc_ref[...] = jnp.zeros_like(acc_ref)
        qh = jnp.dot(x.astype(BF16), wq_ref[...], preferred_element_type=F32)
        for h in range(PEER_HEADS):
            qh_ref[h] = qh[:, 2 * h * PEER_KEY_DIM:2 * (h + 1) * PEER_KEY_DIM].astype(BF16)
        refs = (qh_ref, sk_ref, lr_ref, e1_ref, rank2_ref, e2_ref)
        for lt in range(tm // LANES):
            lanes = slice(lt * LANES, (lt + 1) * LANES)

            def route(h, carry, lanes=lanes):
                tie = _peer_route(h, lanes, *refs, tie_safe=False)

                @pl.when(tie)
                def _():
                    _peer_route(h, lanes, *refs, tie_safe=True)
                return carry

            lax.fori_loop(0, PEER_HEADS, route, 0)

    ht = jnp.dot(u_ref[...], xt_ref[...], preferred_element_type=F32)
    gh = _gelu(ht).astype(BF16)
    wh = []
    for al in range(a_per_block):
        a = kb * a_per_block + al
        w = jnp.zeros((PEER_NKEYS, tm), BF16)
        for h in range(PEER_HEADS):
            lr = jnp.broadcast_to(lr_ref[h, pl.ds(a, 1), :].astype(BF16), (PEER_NKEYS, tm))
            e1 = jnp.broadcast_to(e1_ref[h, pl.ds(a, 1), :].astype(BF16), (PEER_NKEYS, tm))
            w = w + jnp.where(rank2_ref[h] < lr, e2_ref[h], jnp.zeros_like(w)) * e1
        wh.append(w * gh[al * PEER_NKEYS:(al + 1) * PEER_NKEYS])
    acc_ref[...] += jnp.dot(vt_ref[...], jnp.concatenate(wh, axis=0), preferred_element_type=F32)

    @pl.when(kb == pl.num_programs(1) - 1)
    def _():
        y_ref[...] = _layer_norm(DN_ALPHA * x_ref[...] + acc_ref[...].T, g_ref[...], b_ref[...])


def _peer(x, wq_bf, sk_bf, u_bf, vt_bf, g, b):
    n_tokens = x.shape[0]
    n = -(-n_tokens // LANES) * LANES
    x = jnp.pad(x, ((0, n - n_tokens), (0, 0)))
    tm = ROW_TILE if n % ROW_TILE == 0 else LANES
    n_exp = u_bf.shape[0]
    eb = PEER_EXPERT_BLOCK
    const2 = lambda a: pl.BlockSpec(a.shape, lambda i, k: (0, 0))
    return pl.pallas_call(
        _peer_kernel,
        grid=(n // tm, n_exp // eb),
        in_specs=[pl.BlockSpec((tm, D_MODEL), lambda i, k: (i, 0)),
                  const2(wq_bf),
                  pl.BlockSpec(sk_bf.shape, lambda i, k: (0, 0, 0, 0)),
                  pl.BlockSpec((eb, D_MODEL), lambda i, k: (k, 0)),
                  pl.BlockSpec((D_MODEL, eb), lambda i, k: (0, k)),
                  const2(g), const2(b)],
        out_specs=pl.BlockSpec((tm, D_MODEL), lambda i, k: (i, 0)),
        out_shape=jax.ShapeDtypeStruct((n, D_MODEL), F32),
        scratch_shapes=[pltpu.VMEM((D_MODEL, tm), BF16),
                        pltpu.VMEM((PEER_HEADS, tm, 2 * PEER_KEY_DIM), BF16),
                        pltpu.VMEM((PEER_HEADS, PEER_NKEYS, tm), F32),
                        pltpu.VMEM((PEER_HEADS, PEER_NKEYS, tm), F32),
                        pltpu.VMEM((PEER_HEADS, PEER_NKEYS, tm), BF16),
                        pltpu.VMEM((PEER_HEADS, PEER_NKEYS, tm), BF16),
                        pltpu.VMEM((D_MODEL, tm), F32)],
        compiler_params=_params("parallel", "arbitrary"),
        name="peer_ln3",
    )(x, wq_bf, sk_bf, u_bf, vt_bf, g, b)[:n_tokens]


def kernel(x_prompt, x_sample, cache_k_cmp, cache_v_cmp, cache_k_slc, cache_v_slc, cache_k_win, cache_v_win,
           state_hgrn, cache_mem_k, cache_mem_v, page_table, mem_prompt, w_in, hgrn_lb_logits, hgrn_norm_g,
           cmp_k_w1, cmp_k_b1, cmp_k_w2, cmp_k_b2, cmp_v_w1, cmp_v_b1, cmp_v_w2, cmp_v_b2, w_out, ln1_g, ln1_b,
           w_mem_q, w_mem_k, w_mem_v, w_mem_o, ln2_g, ln2_b, peer_w_q, peer_sub_keys, peer_u, peer_v,
           ln3_g, ln3_b):
    bsz, seq, _ = x_prompt.shape
    db, dt, _ = x_sample.shape
    dtp = -(-dt // SUBLANES) * SUBLANES
    layer = 0

    w_in_pad = jnp.pad(w_in[layer], ((0, 0), (0, sum(IN_GROUP_WIDTHS) - IN_COLS))).astype(BF16)
    kwts = _cmp_weights(cmp_k_w1[layer], cmp_k_b1[layer], cmp_k_w2[layer], cmp_k_b2[layer])
    vwts = _cmp_weights(cmp_v_w1[layer], cmp_v_b1[layer], cmp_v_w2[layer], cmp_v_b2[layer])
    w_out_bf = w_out[layer].astype(BF16)
    wmq, wmk, wmv, wmo = (w[layer].astype(BF16) for w in (w_mem_q, w_mem_k, w_mem_v, w_mem_o))
    pwq = peer_w_q[layer].astype(BF16)
    psk = peer_sub_keys[layer].astype(BF16)
    pu = peer_u[layer].astype(BF16)
    pvt = peer_v[layer].astype(BF16).T
    vec = lambda a: a[layer].reshape(1, D_MODEL)
    g1, b1, g2, b2, g3, b3 = (vec(a) for a in (ln1_g, ln1_b, ln2_g, ln2_b, ln3_g, ln3_b))
    norm_g = hgrn_norm_g[layer]

    n_p = bsz * seq
    xp = x_prompt.reshape(n_p, D_MODEL)
    zh, nq, kc, vc, ks, vs, kw, vw, ngate = _inproj(xp, w_in_pad)
    per_b = lambda a: a.reshape(bsz, seq, a.shape[-1])
    chunk = min(HG_CHUNK, seq)
    o_h, s_p = _hgrn(per_b(zh), hgrn_lb_logits, norm_g, jnp.zeros((bsz, HG_HEADS, HG_D, HG_D), F32),
                     chunk=chunk, step_tokens=min(HG_STEP_TOKENS, seq), t_valid=chunk)
    kblk, vblk = _cmp_prompt(per_b(kc), per_b(vc), kwts, vwts)
    o_n = _nsa_prompt(per_b(nq), per_b(ngate), kblk, vblk, per_b(ks), per_b(vs), per_b(kw), per_b(vw))
    x1 = _outproj(o_h.reshape(n_p, HG_WIDTH), o_n.reshape(n_p, NSA_WIDTH), xp, w_out_bf, g1, b1)
    n_mem = mem_prompt.shape[1]
    mem_k, mem_v = _memkv(mem_prompt.reshape(bsz * n_mem, D_MODEL), wmk, wmv)
    mem_k = mem_k.reshape(bsz, n_mem, D_MODEL)
    mem_v = mem_v.reshape(bsz, n_mem, D_MODEL)
    x2 = _memattn(x1.reshape(bsz, seq, D_MODEL), mem_k, mem_v, wmq, wmo, g2, b2, nb=1, tm=min(ROW_TILE, seq))
    y_p = _peer(x2.reshape(n_p, D_MODEL), pwq, psk, pu, pvt, g3, b3).reshape(bsz, seq, D_MODEL)

    kv5 = lambda a: a.reshape(1, bsz, seq, NSA_GROUPS, NSA_HD)
    n_win = min(WINDOW, seq)
    win5 = lambda a: a.reshape(bsz, seq, NSA_GROUPS, NSA_HD)[None, :, seq - n_win:]
    mem5 = lambda a: a.reshape(1, bsz, n_mem, MEM_HEADS, MEM_HD)

    n_s = db * dtp
    xs = jnp.pad(x_sample, ((0, 0), (0, dtp - dt), (0, 0))).reshape(n_s, D_MODEL)
    zh, nq, kc_s, vc_s, ks_s, vs_s, kw_s, vw_s, ngate = _inproj(xs, w_in_pad)
    per_s = lambda a: a.reshape(db, dtp, a.shape[-1])
    o_h, s_s = _hgrn(per_s(zh), hgrn_lb_logits, norm_g, state_hgrn[layer],
                     chunk=dtp, step_tokens=dtp, t_valid=dt)
    o_n = _nsa_sample(page_table, per_s(nq), per_s(ngate), per_s(ks_s), per_s(vs_s), per_s(kw_s), per_s(vw_s),
                      jnp.transpose(cache_k_win[layer], (0, 2, 3, 1)), jnp.transpose(cache_v_win[layer], (0, 2, 3, 1)),
                      kwts, vwts, cache_k_cmp[layer], cache_v_cmp[layer], cache_k_slc[layer], cache_v_slc[layer],
                      t_valid=dt)
    x1 = _outproj(o_h.reshape(n_s, HG_WIDTH), o_n.reshape(n_s, NSA_WIDTH), xs, w_out_bf, g1, b1)
    nb = min(4, db)
    x2 = _memattn(x1.reshape(db, dtp, D_MODEL), cache_mem_k[layer].reshape(db, n_mem, D_MODEL),
                  cache_mem_v[layer].reshape(db, n_mem, D_MODEL), wmq, wmo, g2, b2, nb=nb, tm=dtp)
    y_s = _peer(x2.reshape(n_s, D_MODEL), pwq, psk, pu, pvt, g3, b3).reshape(db, dtp, D_MODEL)[:, :dt]
    skv5 = lambda a: a.reshape(db, dtp, NSA_GROUPS, NSA_HD)[None, :, :dt]

    return (y_p, y_s, kv5(kc), kv5(vc), kv5(ks), kv5(vs), win5(kw), win5(vw), s_p[None], mem5(mem_k), mem5(mem_v),
            skv5(kc_s), skv5(vc_s), skv5(ks_s), skv5(vs_s), skv5(kw_s), skv5(vw_s), s_s[None])
```

```python
import functools

import numpy as np
import jax
import jax.numpy as jnp
from jax import lax
from jax.experimental import pallas as pl
from jax.experimental.pallas import tpu as pltpu

F32 = jnp.float32
BF16 = jnp.bfloat16

D_MODEL = 1024
HG_HEADS = 4
HG_D = 128
HG_WIDTH = HG_HEADS * HG_D
NSA_HEADS = 8
NSA_GROUPS = 2
NSA_HPG = NSA_HEADS // NSA_GROUPS
NSA_HD = 64
NSA_WIDTH = NSA_HEADS * NSA_HD
KV_WIDTH = NSA_GROUPS * NSA_HD
CMP_BLOCK = 32
CMP_STRIDE = 16
CMP_HIDDEN = 128
SEL_BLOCK = 64
N_SEL = 16
WINDOW = 512
PAGE_SIZE = 128
MEM_HEADS = 4
MEM_HD = D_MODEL // MEM_HEADS
PEER_HEADS = 8
PEER_NKEYS = 128
PEER_KEY_DIM = 128
PEER_TOPK = 16
DN_ALPHA = 2.0 ** 0.25
LN_EPS = 1e-5
NEG_INF = -1e30
LOG2E = 1.4426950408889634
ALIBI_SLOPES = tuple(2.0 ** (-8.0 * (h + 1) / NSA_HEADS) for h in range(NSA_HEADS))

LANES = 128
SUBLANES = 8
VMEM_LIMIT_BYTES = 56 * 1024 * 1024

HG_CHUNK = 32
HG_STEP_TOKENS = 256
NSA_Q_TILE = 128
ROW_TILE = 512
PEER_EXPERT_BLOCK = 1024
PEER_ROUTE_GROUP = 4
SLC_KEY_CHUNK = 512

IN_GROUP_WIDTHS = (4 * HG_WIDTH, NSA_WIDTH) + (KV_WIDTH,) * 6 + (LANES,)
IN_COLS = 4 * HG_WIDTH + NSA_WIDTH + 6 * KV_WIDTH + 3 * NSA_HEADS


def _params(*sem):
    return pltpu.CompilerParams(dimension_semantics=sem, vmem_limit_bytes=VMEM_LIMIT_BYTES)


def _gelu(x):
    return 0.5 * x * (1.0 + jnp.tanh(0.7978845608028654 * (x + 0.044715 * (x * x * x))))


def _layer_norm(y, g, b):
    mu = jnp.mean(y, -1, keepdims=True)
    yc = y - mu
    var = jnp.mean(yc * yc, -1, keepdims=True)
    return yc * lax.rsqrt(var + LN_EPS) * g + b


def _dot(a, b):
    return jnp.dot(a.astype(BF16), b.astype(BF16), preferred_element_type=F32)


def _dot_nt(a, b):
    return lax.dot_general(a.astype(BF16), b.astype(BF16), (((1,), (1,)), ((), ())),
                           preferred_element_type=F32)


def _dot_tn(a, b):
    return lax.dot_general(a.astype(BF16), b.astype(BF16), (((0,), (0,)), ((), ())),
                           preferred_element_type=F32)


def _masked_exp(s, mask):
    sm = jnp.where(mask, s, NEG_INF)
    m = jnp.max(sm, -1, keepdims=True)
    e = jnp.where(mask, jnp.exp(sm - m), 0.0)
    return e, jnp.sum(e, -1, keepdims=True)


def _safe_inv(d):
    return jnp.where(d > 0.0, 1.0 / jnp.where(d > 0.0, d, 1.0), 0.0)


def _inproj_kernel(x_ref, w_ref, *out_refs):
    xb = x_ref[...].astype(BF16)
    off = 0
    for o_ref in out_refs:
        width = o_ref.shape[-1]
        o_ref[...] = jnp.dot(xb, w_ref[:, off:off + width], preferred_element_type=F32)
        off += width


def _inproj(x, w_pad):
    n = x.shape[0]
    tm = min(ROW_TILE, n)
    return pl.pallas_call(
        _inproj_kernel,
        grid=(n // tm,),
        in_specs=[pl.BlockSpec((tm, D_MODEL), lambda i: (i, 0)),
                  pl.BlockSpec(w_pad.shape, lambda i: (0, 0))],
        out_specs=[pl.BlockSpec((tm, w), lambda i: (i, 0)) for w in IN_GROUP_WIDTHS],
        out_shape=[jax.ShapeDtypeStruct((n, w), F32) for w in IN_GROUP_WIDTHS],
        compiler_params=_params("parallel"),
        name="inproj",
    )(x, w_pad)


def _hgrn_kernel(zh_ref, lbl_ref, ng_ref, s0_ref, tri_ref, o_ref, sout_ref, st_ref, obuf_ref,
                 *, chunk, t_valid):
    ti = pl.program_id(1)
    step_tokens = zh_ref.shape[1]

    @pl.when(ti == 0)
    def _():
        for h in range(HG_HEADS):
            st_ref[h] = s0_ref[0, h].T

    logits = lbl_ref[...]
    le = jnp.exp(logits - jnp.max(logits, axis=0, keepdims=True))
    lb_all = le[0:1] / jnp.sum(le, axis=0, keepdims=True)
    norm_g = ng_ref[...]
    row = lax.broadcasted_iota(jnp.int32, (chunk, 1), 0)

    def do_chunk(c, carry):
        r0 = pl.multiple_of(c * chunk, chunk)
        rows = pl.ds(r0, chunk)
        for h in range(HG_HEADS):
            cs = h * HG_D
            q = zh_ref[0, rows, cs:cs + HG_D]
            zf = zh_ref[0, rows, HG_WIDTH + cs:HG_WIDTH + cs + HG_D]
            v = zh_ref[0, rows, 2 * HG_WIDTH + cs:2 * HG_WIDTH + cs + HG_D]
            zg = zh_ref[0, rows, 3 * HG_WIDTH + cs:3 * HG_WIDTH + cs + HG_D]
            lb = lb_all[:, cs:cs + HG_D]
            lf = jnp.log(lb + (1.0 - lb) * jax.nn.sigmoid(zf))
            k = (1.0 - lb) * jax.nn.sigmoid(-zf)
            if t_valid < chunk:
                lf = jnp.where(row < t_valid, lf, 0.0)
                k = jnp.where(row < t_valid, k, 0.0)
            if chunk <= SUBLANES:
                g = jnp.where(row >= 0, lf[0:1], 0.0)
                for t in range(1, chunk):
                    g = g + jnp.where(row >= t, lf[t:t + 1], 0.0)
            else:
                g = jnp.dot(tri_ref[...], lf, precision=lax.Precision.HIGHEST, preferred_element_type=F32)
            st = st_ref[h]
            o_inter = _dot_nt(q * jnp.exp(g), st)
            for t in range(chunk):
                r8 = -(-(t + 1) // SUBLANES) * SUBLANES
                rel = jnp.where(row[:r8] <= t, g[t:t + 1] - g[:r8], NEG_INF)
                p = (q[t:t + 1] * k[:r8]) * jnp.exp(rel)
                w = jnp.sum(p, axis=-1, keepdims=True)
                obuf_ref[t:t + 1, :] = jnp.sum(w * v[:r8], axis=0, keepdims=True)
            o = o_inter + obuf_ref[...]
            g_end = g[chunk - 1:chunk]
            kk = k * jnp.exp(g_end - g)
            st_ref[h] = st * jnp.exp(g_end) + _dot_tn(v, kk)
            o = o * lax.rsqrt(jnp.mean(o * o, -1, keepdims=True) + LN_EPS) * norm_g
            o_ref[0, rows, cs:cs + HG_D] = o * jax.nn.sigmoid(zg)
        return carry

    lax.fori_loop(0, step_tokens // chunk, do_chunk, 0)

    @pl.when(ti == pl.num_programs(1) - 1)
    def _():
        for h in range(HG_HEADS):
            sout_ref[0, h] = st_ref[h].T


def _hgrn(zh, lb_logits, norm_g, s0, *, chunk, step_tokens, t_valid):
    b, t, _ = zh.shape
    tri = jnp.asarray(np.tril(np.ones((chunk, chunk), np.float32)))
    kern = functools.partial(_hgrn_kernel, chunk=chunk, t_valid=t_valid)
    return pl.pallas_call(
        kern,
        grid=(b, t // step_tokens),
        in_specs=[pl.BlockSpec((1, step_tokens, 4 * HG_WIDTH), lambda i, j: (i, j, 0)),
                  pl.BlockSpec(lb_logits.shape, lambda i, j: (0, 0)),
                  pl.BlockSpec((1, HG_D), lambda i, j: (0, 0)),
                  pl.BlockSpec((1, HG_HEADS, HG_D, HG_D), lambda i, j: (i, 0, 0, 0)),
                  pl.BlockSpec((chunk, chunk), lambda i, j: (0, 0))],
        out_specs=[pl.BlockSpec((1, step_tokens, HG_WIDTH), lambda i, j: (i, j, 0)),
                   pl.BlockSpec((1, HG_HEADS, HG_D, HG_D), lambda i, j: (i, 0, 0, 0))],
        out_shape=[jax.ShapeDtypeStruct((b, t, HG_WIDTH), F32),
                   jax.ShapeDtypeStruct((b, HG_HEADS, HG_D, HG_D), F32)],
        scratch_shapes=[pltpu.VMEM((HG_HEADS, HG_D, HG_D), F32), pltpu.VMEM((chunk, HG_D), F32)],
        compiler_params=_params("parallel", "arbitrary"),
        name="hgrn",
    )(zh, lb_logits, norm_g.reshape(1, HG_D), s0, tri)


def _cmp_blocks(view, w1_ref, b1_ref, w2_ref, b2_ref):
    return _cmp_finish(_dot(view, w1_ref[...]), b1_ref, w2_ref, b2_ref)


def _cmp_blocks_from_rows(rows_ref, w1_ref, b1_ref, w2_ref, b2_ref):
    n_sub = rows_ref.shape[0] // CMP_STRIDE
    a = None
    for p in range(CMP_STRIDE):
        part = _dot(rows_ref[pl.ds(p, n_sub, stride=CMP_STRIDE), :], w1_ref[p * KV_WIDTH:(p + 1) * KV_WIDTH, :])
        a = part if a is None else a + part
    return _cmp_finish(a, b1_ref, w2_ref, b2_ref)


def _cmp_finish(a, b1_ref, w2_ref, b2_ref):
    n_sub = a.shape[0]
    hid = []
    for g in range(NSA_GROUPS):
        a0 = a[:, (2 * g) * CMP_HIDDEN:(2 * g + 1) * CMP_HIDDEN]
        a1 = a[:, (2 * g + 1) * CMP_HIDDEN:(2 * g + 2) * CMP_HIDDEN]
        hid.append(a0 + pltpu.roll(a1, n_sub - 1, axis=0))
    h = jnp.concatenate(hid, axis=-1) + b1_ref[...]
    return _dot(_gelu(h), w2_ref[...]) + b2_ref[...]


def _cmp_kernel(kv_ref, vv_ref, kw1, kb1, kw2, kb2, vw1, vb1, vw2, vb2, ko_ref, vo_ref):
    ko_ref[0] = _cmp_blocks(kv_ref[0], kw1, kb1, kw2, kb2)
    vo_ref[0] = _cmp_blocks(vv_ref[0], vw1, vb1, vw2, vb2)


def _cmp_weights(w1, b1, w2, b2):
    w1r = w1.reshape(CMP_BLOCK // CMP_STRIDE, CMP_STRIDE, NSA_HD, CMP_HIDDEN)
    big = jnp.zeros((CMP_STRIDE, NSA_GROUPS, NSA_HD, NSA_GROUPS, 2, CMP_HIDDEN), F32)
    for g in range(NSA_GROUPS):
        big = big.at[:, g, :, g, :, :].set(jnp.transpose(w1r, (1, 2, 0, 3)))
    w1b = big.reshape(CMP_STRIDE * KV_WIDTH, NSA_GROUPS * 2 * CMP_HIDDEN).astype(BF16)
    w2b = jnp.zeros((NSA_GROUPS, CMP_HIDDEN, NSA_GROUPS, NSA_HD), F32)
    for g in range(NSA_GROUPS):
        w2b = w2b.at[g, :, g, :].set(w2)
    w2b = w2b.reshape(NSA_GROUPS * CMP_HIDDEN, KV_WIDTH).astype(BF16)
    b1b = jnp.tile(b1, NSA_GROUPS).reshape(1, NSA_GROUPS * CMP_HIDDEN)
    b2b = jnp.tile(b2, NSA_GROUPS).reshape(1, KV_WIDTH)
    return w1b, b1b, w2b, b2b


def _cmp_prompt(kc, vc, kwts, vwts):
    b, t, _ = kc.shape
    n_sub = t // CMP_STRIDE
    view = lambda a: a.reshape(b, n_sub, CMP_STRIDE * KV_WIDTH)
    wspecs = [pl.BlockSpec(w.shape, lambda i: (0, 0)) for w in kwts + vwts]
    return pl.pallas_call(
        _cmp_kernel,
        grid=(b,),
        in_specs=[pl.BlockSpec((1, n_sub, CMP_STRIDE * KV_WIDTH), lambda i: (i, 0, 0))] * 2 + wspecs,
        out_specs=[pl.BlockSpec((1, n_sub, KV_WIDTH), lambda i: (i, 0, 0))] * 2,
        out_shape=[jax.ShapeDtypeStruct((b, n_sub, KV_WIDTH), F32)] * 2,
        compiler_params=_params("parallel"),
        name="cmp_prompt",
    )(view(kc), view(vc), *kwts, *vwts)


def _cmp_to_slc(n_cmp, n_slc):
    cs = np.arange(n_cmp)[:, None] * CMP_STRIDE
    ss = np.arange(n_slc)[None, :] * SEL_BLOCK
    shared = np.clip(np.minimum(cs + CMP_BLOCK, ss + SEL_BLOCK) - np.maximum(cs, ss), 0, None)
    return (shared / CMP_STRIDE).astype(np.float32)


def _softmax2(s):
    m = jnp.max(s, -1, keepdims=True)
    e = jnp.exp2(s - m)
    inv = jnp.where(m > 0.5 * NEG_INF, 1.0 / jnp.sum(e, -1, keepdims=True), 0.0)
    return e, inv


def _nsa_prompt_kernel(q_ref, gate_ref, kb_ref, vb_ref, ks_ref, vs_ref, kw_ref, vw_ref, mt_ref, ex_ref,
                       o_ref, bias_ref, *, n_cmp, n_var):
    tq = q_ref.shape[1]
    t_all = ks_ref.shape[1]
    n_sub = kb_ref.shape[1]
    n_slc = mt_ref.shape[0]
    tile = pl.program_id(1)
    t0 = tile * tq
    q = q_ref[0] * (NSA_HD ** -0.5 * LOG2E)
    slopes = tuple(s * LOG2E for s in ALIBI_SLOPES)
    gates = jax.nn.sigmoid(gate_ref[0])
    qpos = t0 + lax.broadcasted_iota(jnp.int32, (tq, 1), 0)

    def stack_heads(g):
        return jnp.concatenate([q[:, h * NSA_HD:(h + 1) * NSA_HD] for h in range(g * NSA_HPG, (g + 1) * NSA_HPG)],
                               axis=0)

    cidx = lax.broadcasted_iota(jnp.int32, (1, n_sub), 1)
    kend =cidx * CMP_STRIDE + CMP_BLOCK - 1
    bias_c = jnp.where((qpos - kend >= 0) & (cidx < n_cmp), 0.0, NEG_INF)
    kend_rel = (kend - t0).astype(F32)
    nidx = lax.broadcasted_iota(jnp.int32, (n_slc, 1), 0)
    tpos = t0 + lax.broadcasted_iota(jnp.int32, (1, tq), 1)
    elig = nidx * SEL_BLOCK <= tpos
    force = (nidx == tpos // SEL_BLOCK) | (nidx == 0)
    causal = qpos - lax.broadcasted_iota(jnp.int32, (1, t_all), 1) >= 0
    span = min(WINDOW + tq, t_all)
    start = pl.multiple_of(jnp.clip(t0 - WINDOW, 0, t_all - span), tq)
    kpos_w = start + lax.broadcasted_iota(jnp.int32, (1, span), 1)
    bias_w = jnp.where((qpos - kpos_w >= 0) & (qpos - kpos_w < WINDOW), 0.0, NEG_INF)
    kpos_w_rel = (kpos_w - t0).astype(F32)

    for g in range(NSA_GROUPS):
        gs = g * NSA_HD
        kb = kb_ref[0, :, gs:gs + NSA_HD]
        vb = vb_ref[0, :, gs:gs + NSA_HD]
        kw = kw_ref[0, pl.ds(start, span), gs:gs + NSA_HD]
        vw = vw_ref[0, pl.ds(start, span), gs:gs + NSA_HD]
        heads = range(g * NSA_HPG, (g + 1) * NSA_HPG)
        qg = stack_heads(g)
        e, inv = _softmax2(_dot_nt(qg, kb) + jnp.concatenate([bias_c + slopes[h] * kend_rel for h in heads], axis=0))
        p = e * inv
        o_cmp = _dot(p, vb)
        psum = p[0:tq]
        for j in range(1, NSA_HPG):
            psum = psum + p[j * tq:(j + 1) * tq]
        e, inv = _softmax2(_dot_nt(qg, kw) + jnp.concatenate([bias_w + slopes[h] * kpos_w_rel for h in heads], axis=0))
        o_win = _dot(e, vw) * inv
        for j, h in enumerate(heads):
            rs = slice(j * tq, (j + 1) * tq)
            o_ref[0, :, h * NSA_HD:(h + 1) * NSA_HD] = (
                gates[:, 3 * h:3 * h + 1] * o_cmp[rs] + gates[:, 3 * h + 2:3 * h + 3] * o_win[rs])
        imp_t = lax.dot_general(mt_ref[...], psum, (((1,), (1,)), ((), ())),
                                precision=lax.Precision.HIGHEST, preferred_element_type=F32)
        val = jnp.where(force, jnp.inf, jnp.where(elig, imp_t, -jnp.inf))
        cnt = jnp.zeros((n_slc, tq), F32)
        for m in range(n_slc):
            other = val[m:m + 1]
            tie = jnp.where(nidx > m, 1.0, 0.0)
            cnt = cnt + jnp.where(other > val, 1.0, jnp.where(other == val, tie, 0.0))
        sel_t = jnp.where(cnt < float(min(N_SEL, n_slc)), 1.0, 0.0)
        sel_keys = _dot(sel_t.T, ex_ref[...])
        bias_ref[g] = jnp.where((sel_keys > 0.5) & causal, 0.0, NEG_INF)

    tiles_per_var = (t_all // tq) // n_var
    for v in range(n_var):
        n_keys = (v + 1) * tiles_per_var * tq

        @pl.when(tile // tiles_per_var == v)
        def _(n_keys=n_keys):
            kpos_rel = (lax.broadcasted_iota(jnp.int32, (1, n_keys), 1) - t0).astype(F32)
            for g in range(NSA_GROUPS):
                gs = g * NSA_HD
                ks = ks_ref[0, 0:n_keys, gs:gs + NSA_HD]
                vs = vs_ref[0, 0:n_keys, gs:gs + NSA_HD]
                bias_s = bias_ref[g, :, 0:n_keys]
                heads = range(g * NSA_HPG, (g + 1) * NSA_HPG)
                bias = jnp.concatenate([bias_s + slopes[h] * kpos_rel for h in heads], axis=0)
                e, inv = _softmax2(_dot_nt(stack_heads(g), ks) + bias)
                o_slc = _dot(e, vs) * inv
                for j, h in enumerate(heads):
                    o_ref[0, :, h * NSA_HD:(h + 1) * NSA_HD] += gates[:, 3 * h + 1:3 * h + 2] * o_slc[j * tq:(j + 1) * tq]


def _nsa_prompt(nq, ng, kblk, vblk, ks, vs, kw, vw):
    b, t, _ = nq.shape
    tq = min(NSA_Q_TILE, t)
    n_sub = kblk.shape[1]
    n_cmp = n_sub - CMP_BLOCK // CMP_STRIDE + 1
    n_slc = -(-t // SEL_BLOCK)
    n_var = max(1, t // SLC_KEY_CHUNK)
    assert (t // tq) % n_var == 0
    mt = np.zeros((n_slc, n_sub), np.float32)
    mt[:, :n_cmp] = _cmp_to_slc(n_cmp, n_slc).T
    expand = (np.arange(t)[None, :] // SEL_BLOCK == np.arange(n_slc)[:, None]).astype(np.float32)
    full = lambda w: pl.BlockSpec((1, t, w), lambda i, j: (i, 0, 0))
    return pl.pallas_call(
        functools.partial(_nsa_prompt_kernel, n_cmp=n_cmp, n_var=n_var),
        grid=(b, t // tq),
        in_specs=[pl.BlockSpec((1, tq, NSA_WIDTH), lambda i, j: (i, j, 0)),
                  pl.BlockSpec((1, tq, LANES), lambda i, j: (i, j, 0)),
                  pl.BlockSpec((1, n_sub, KV_WIDTH), lambda i, j: (i, 0, 0)),
                  pl.BlockSpec((1, n_sub, KV_WIDTH), lambda i, j: (i, 0, 0)),
                  full(KV_WIDTH), full(KV_WIDTH), full(KV_WIDTH), full(KV_WIDTH),
                  pl.BlockSpec(mt.shape, lambda i, j: (0, 0)),
                  pl.BlockSpec(expand.shape, lambda i, j: (0, 0))],
        out_specs=pl.BlockSpec((1, tq, NSA_WIDTH), lambda i, j: (i, j, 0)),
        out_shape=jax.ShapeDtypeStruct((b, t, NSA_WIDTH), F32),
        scratch_shapes=[pltpu.VMEM((NSA_GROUPS, tq, t), F32)],
        compiler_params=_params("parallel", "arbitrary"),
        name="nsa_prompt",
    )(nq, ng, kblk, vblk, ks, vs, kw, vw, jnp.asarray(mt), jnp.asarray(expand, dtype=BF16))


def _nsa_sample_kernel(pt_ref, q_ref, gate_ref, ksn_ref, vsn_ref, kwn_ref, vwn_ref, wk_ref, wv_ref,
                       kw1, kb1, kw2, kb2, vw1, vb1, vw2, vb2, m_ref, ex_ref,
                       kc_pool, vc_pool, ks_pool, vs_pool,
                       o_ref, kc_buf, vc_buf, ks_buf, vs_buf, sem, *, t_valid, n_slc):
    b = pl.program_id(0)
    n_pages = pt_ref.shape[1]
    tp = q_ref.shape[1]
    past = n_pages * PAGE_SIZE
    n_sub = past // CMP_STRIDE
    n_cmp = n_sub - CMP_BLOCK // CMP_STRIDE + 1
    n_past_blocks = past // SEL_BLOCK

    pools = ((kc_pool, kc_buf, False), (vc_pool, vc_buf, False), (ks_pool, ks_buf, True), (vs_pool, vs_buf, True))

    def page_copy(i, j):
        pool, buf, rows_on_lanes = pools[i]
        rows = pl.ds(pl.multiple_of(j * PAGE_SIZE, PAGE_SIZE), PAGE_SIZE)
        dst = buf.at[:, :, rows] if rows_on_lanes else buf.at[rows]
        return pltpu.make_async_copy(pool.at[pt_ref[b, j]], dst, sem.at[i])

    def start_pages(which):
        def body(j, carry):
            for i in which:
                page_copy(i, j).start()
            return carry
        lax.fori_loop(0, n_pages, body, 0)

    def wait_pages(which):
        def body(j, carry):
            for i in which:
                page_copy(i, j).wait()
            return carry
        lax.fori_loop(0, n_pages, body, 0)

    start_pages((0, 1))
    start_pages((2, 3))
    wait_pages((0, 1))
    kblk = _cmp_blocks_from_rows(kc_buf, kw1, kb1, kw2, kb2)
    vblk = _cmp_blocks_from_rows(vc_buf, vw1, vb1, vw2, vb2)
    wait_pages((2, 3))

    q = q_ref[0] * (NSA_HD ** -0.5)
    gates = jax.nn.sigmoid(gate_ref[0])
    trow = lax.broadcasted_iota(jnp.int32, (tp, 1), 0)
    trow_q = jnp.concatenate([trow] * NSA_HPG, axis=0)
    qpos_q = past + trow_q
    cidx = lax.broadcasted_iota(jnp.int32, (1, n_sub), 1)
    dist_c = qpos_q - (cidx * CMP_STRIDE + CMP_BLOCK - 1)
    mask_c = (dist_c >= 0) & (cidx < n_cmp)
    dist_cf = dist_c.astype(F32)
    n_rows = m_ref.shape[0]
    nidx = lax.broadcasted_iota(jnp.int32, (n_rows, 1), 0)
    nidx_f = nidx.astype(F32)
    qpos_t = past + lax.broadcasted_iota(jnp.int32, (1, tp), 1)
    elig = (nidx * SEL_BLOCK <= qpos_t) & (nidx < n_slc)
    force = ((nidx == qpos_t // SEL_BLOCK) | (nidx == 0)) & (nidx < n_slc)
    eye = lax.broadcasted_iota(jnp.int32, (tp, tp), 0) == lax.broadcasted_iota(jnp.int32, (tp, tp), 1)
    kpos = lax.broadcasted_iota(jnp.int32, (1, past), 1)
    dist_pf = (qpos_q - kpos).astype(F32)
    rnew = lax.broadcasted_iota(jnp.int32, (1, tp), 1)
    dist_n = trow_q - rnew
    mask_n = (dist_n >= 0) & (rnew < t_valid)
    dist_nf = dist_n.astype(F32)
    n_buf = wk_ref.shape[3]
    dist_b = qpos_q - (past - n_buf + lax.broadcasted_iota(jnp.int32, (1, n_buf), 1))
    mask_b = (dist_b >= 0) & (dist_b < WINDOW)
    dist_bf = dist_b.astype(F32)

    for g in range(NSA_GROUPS):
        gs = g * NSA_HD
        heads = range(g * NSA_HPG, (g + 1) * NSA_HPG)
        qg = jnp.concatenate([q[:, h * NSA_HD:(h + 1) * NSA_HD] for h in heads], axis=0)
        slope = jnp.concatenate([jnp.full((tp, 1), ALIBI_SLOPES[h], F32) for h in heads], axis=0)
        s = _dot_nt(qg, kblk[:, gs:gs + NSA_HD]) - slope * dist_cf
        e, d = _masked_exp(s, mask_c)
        p = e * _safe_inv(d)
        o_cmp = _dot(p, vblk[:, gs:gs + NSA_HD])
        psum = p[0:tp]
        for j in range(1, NSA_HPG):
            psum = psum + p[j * tp:(j + 1) * tp]
        imp = lax.dot_general(m_ref[...], psum, (((1,), (1,)), ((), ())),
                              precision=lax.Precision.HIGHEST, preferred_element_type=F32)
        val = jnp.where(force, jnp.inf, jnp.where(elig, imp, -jnp.inf))
        sel = jnp.zeros((n_rows, tp), F32)
        for _ in range(min(N_SEL, n_slc)):
            m = jnp.max(val, axis=0, keepdims=True)
            first = jnp.min(jnp.where(val == m, nidx_f, float(n_rows)), axis=0, keepdims=True)
            pick = nidx_f == first
            sel = jnp.where(pick, 1.0, sel)
            val = jnp.where(pick, -jnp.inf, val)
        sel = jnp.where(nidx < n_slc, sel, 0.0)
        sel_keys = _dot_tn(sel[:n_past_blocks], ex_ref[...])
        mask_p = jnp.concatenate([sel_keys] * NSA_HPG, axis=0) > 0.5
        sel_new = jnp.sum(jnp.where(eye, sel[n_past_blocks:n_past_blocks + 1], 0.0), axis=-1, keepdims=True)
        sel_new = jnp.concatenate([sel_new] * NSA_HPG, axis=0) > 0.5
        s_p = _dot(qg, ks_buf[g]) - slope * dist_pf
        s_n = _dot_nt(qg, ksn_ref[0, :, gs:gs + NSA_HD]) - slope * dist_nf
        mask_sn = mask_n & sel_new
        mx = jnp.maximum(jnp.max(jnp.where(mask_p, s_p, NEG_INF), -1, keepdims=True),
                         jnp.max(jnp.where(mask_sn, s_n, NEG_INF), -1, keepdims=True))
        e_p = jnp.where(mask_p, jnp.exp(jnp.where(mask_p, s_p, NEG_INF) - mx), 0.0)
        e_n = jnp.where(mask_sn, jnp.exp(jnp.where(mask_sn, s_n, NEG_INF) - mx), 0.0)
        den = jnp.sum(e_p, -1, keepdims=True) + jnp.sum(e_n, -1, keepdims=True)
        o_slc = (_dot_nt(e_p, vs_buf[g]) + _dot(e_n, vsn_ref[0, :, gs:gs + NSA_HD])) * _safe_inv(den)
        s_b = _dot(qg, wk_ref[0, g]) - slope * dist_bf
        s_n = _dot_nt(qg, kwn_ref[0, :, gs:gs + NSA_HD]) - slope * dist_nf
        mx = jnp.maximum(jnp.max(jnp.where(mask_b, s_b, NEG_INF), -1, keepdims=True),
                         jnp.max(jnp.where(mask_n, s_n, NEG_INF), -1, keepdims=True))
        e_b = jnp.where(mask_b, jnp.exp(jnp.where(mask_b, s_b, NEG_INF) - mx), 0.0)
        e_n = jnp.where(mask_n, jnp.exp(jnp.where(mask_n, s_n, NEG_INF) - mx), 0.0)
        den = jnp.sum(e_b, -1, keepdims=True) + jnp.sum(e_n, -1, keepdims=True)
        o_win = (_dot_nt(e_b, wv_ref[0, g]) + _dot(e_n, vwn_ref[0, :, gs:gs + NSA_HD])) * _safe_inv(den)
        for j, h in enumerate(heads):
            rs = slice(j * tp, (j + 1) * tp)
            o_ref[0, :, h * NSA_HD:(h + 1) * NSA_HD] = (
                gates[:, 3 * h:3 * h + 1] * o_cmp[rs]
                + gates[:, 3 * h + 1:3 * h + 2] * o_slc[rs]
                + gates[:, 3 * h + 2:3 * h + 3] * o_win[rs])


def _nsa_sample(page_table, nq, ng, ksn, vsn, kwn, vwn, win_k, win_v, kwts, vwts,
                pool_kc, pool_vc, pool_ks, pool_vs, *, t_valid):
    db, tp, _ = nq.shape
    n_pages = page_table.shape[1]
    past = n_pages * PAGE_SIZE
    n_sub = past // CMP_STRIDE
    n_cmp = n_sub - CMP_BLOCK // CMP_STRIDE + 1
    n_slc = -(-(past + t_valid) // SEL_BLOCK)
    n_rows = -(-n_slc // SUBLANES) * SUBLANES
    n_buf = win_k.shape[3]
    m = np.zeros((n_rows, n_sub), np.float32)
    m[:n_slc, :n_cmp] = _cmp_to_slc(n_cmp, n_slc).T
    n_past_blocks = past // SEL_BLOCK
    expand = (np.arange(past)[None, :] // SEL_BLOCK == np.arange(n_past_blocks)[:, None]).astype(np.float32)
    row_view = lambda p: p.reshape(p.shape[0], PAGE_SIZE, KV_WIDTH)
    col_view = lambda p: jnp.transpose(p, (0, 2, 3, 1))
    per_b = lambda r, w: pl.BlockSpec((1, r, w), lambda i, pt: (i, 0, 0))
    const = lambda a: pl.BlockSpec(a.shape, lambda i, pt: (0,) * a.ndim)
    wts = kwts + vwts
    m_j = jnp.asarray(m)
    ex_j = jnp.asarray(expand, dtype=BF16)
    any_spec = pl.BlockSpec(memory_space=pl.ANY)
    grid_spec = pltpu.PrefetchScalarGridSpec(
        num_scalar_prefetch=1,
        grid=(db,),
        in_specs=[per_b(tp, NSA_WIDTH), per_b(tp, LANES)] + [per_b(tp, KV_WIDTH)] * 4
                 + [pl.BlockSpec((1, NSA_GROUPS, NSA_HD, n_buf), lambda i, pt: (i, 0, 0, 0))] * 2
                 + [const(w) for w in wts] + [const(m_j), const(ex_j)]
                 + [any_spec] * 4,
        out_specs=per_b(tp, NSA_WIDTH),
        scratch_shapes=[pltpu.VMEM((past, KV_WIDTH), F32)] * 2
                       + [pltpu.VMEM((NSA_GROUPS, NSA_HD, past), F32)] * 2
                       + [pltpu.SemaphoreType.DMA((4,))],
    )
    return pl.pallas_call(
        functools.partial(_nsa_sample_kernel, t_valid=t_valid, n_slc=n_slc),
        grid_spec=grid_spec,
        out_shape=jax.ShapeDtypeStruct((db, tp, NSA_WIDTH), F32),
        compiler_params=_params("arbitrary"),
        name="nsa_sample",
    )(page_table, nq, ng, ksn, vsn, kwn, vwn, win_k, win_v, *wts, m_j, ex_j,
      row_view(pool_kc), row_view(pool_vc), col_view(pool_ks), col_view(pool_vs))


def _outproj_kernel(oh_ref, on_ref, x_ref, w_ref, g_ref, b_ref, y_ref):
    mix = (jnp.dot(oh_ref[...].astype(BF16), w_ref[:HG_WIDTH, :], preferred_element_type=F32)
           + jnp.dot(on_ref[...].astype(BF16), w_ref[HG_WIDTH:, :], preferred_element_type=F32))
    y_ref[...] = _layer_norm(DN_ALPHA * x_ref[...] + mix, g_ref[...], b_ref[...])


def _outproj(o_h, o_n, x, w_bf, g, b):
    n = x.shape[0]
    tm = min(ROW_TILE, n)
    row = lambda w: pl.BlockSpec((tm, w), lambda i: (i, 0))
    const = lambda a: pl.BlockSpec(a.shape, lambda i: (0, 0))
    return pl.pallas_call(
        _outproj_kernel,
        grid=(n // tm,),
        in_specs=[row(HG_WIDTH), row(NSA_WIDTH), row(D_MODEL), const(w_bf), const(g), const(b)],
        out_specs=row(D_MODEL),
        out_shape=jax.ShapeDtypeStruct((n, D_MODEL), F32),
        compiler_params=_params("parallel"),
        name="outproj_ln1",
    )(o_h, o_n, x, w_bf, g, b)


def _memkv_kernel(m_ref, wk_ref, wv_ref, k_ref, v_ref):
    mb = m_ref[...].astype(BF16)
    for w_ref, o_ref in ((wk_ref, k_ref), (wv_ref, v_ref)):
        res = jnp.dot(mb, w_ref[...], preferred_element_type=F32)
        for h in range(MEM_HEADS):
            o_ref[:, h, :] = res[:, h * MEM_HD:(h + 1) * MEM_HD]


def _memkv(mem, wk_bf, wv_bf):
    n = mem.shape[0]
    tm = min(ROW_TILE, n)
    row = pl.BlockSpec((tm, D_MODEL), lambda i: (i, 0))
    heads = pl.BlockSpec((tm, MEM_HEADS, MEM_HD), lambda i: (i, 0, 0))
    const = lambda a: pl.BlockSpec(a.shape, lambda i: (0, 0))
    return pl.pallas_call(
        _memkv_kernel,
        grid=(n // tm,),
        in_specs=[row, const(wk_bf), const(wv_bf)],
        out_specs=[heads, heads],
        out_shape=[jax.ShapeDtypeStruct((n, MEM_HEADS, MEM_HD), F32)] * 2,
        compiler_params=_params("parallel"),
        name="mem_kv",
    )(mem, wk_bf, wv_bf)


def _memattn_kernel(x_ref, mk_ref, mv_ref, wq_ref, wo_ref, g_ref, b_ref, y_ref, o_buf):
    nb, tm, _ = x_ref.shape
    x = x_ref[...].reshape(nb * tm, D_MODEL)
    q = jnp.dot(x.astype(BF16), wq_ref[...], preferred_element_type=F32) * (MEM_HD ** -0.5)
    for i in range(nb):
        rs = slice(i * tm, (i + 1) * tm)
        for h in range(MEM_HEADS):
            cs = slice(h * MEM_HD, (h + 1) * MEM_HD)
            s = _dot_nt(q[rs, cs], mk_ref[i, :, h, :])
            e = jnp.exp(s - jnp.max(s, -1, keepdims=True))
            o_buf[rs, cs] = _dot(e, mv_ref[i, :, h, :]) / jnp.sum(e, -1, keepdims=True)
    att = jnp.dot(o_buf[...].astype(BF16), wo_ref[...], preferred_element_type=F32)
    y = _layer_norm(DN_ALPHA * x + att, g_ref[...], b_ref[...])
    y_ref[...] = y.reshape(nb, tm, D_MODEL)


def _memattn(x, mem_k, mem_v, wq_bf, wo_bf, g, b, *, nb, tm):
    bsz, t, _ = x.shape
    n_mem = mem_k.shape[1]
    const = lambda a: pl.BlockSpec(a.shape, lambda i, j: (0, 0))
    return pl.pallas_call(
        _memattn_kernel,
        grid=(bsz // nb, t // tm),
        in_specs=[pl.BlockSpec((nb, tm, D_MODEL), lambda i, j: (i, j, 0)),
                  pl.BlockSpec((nb, n_mem, MEM_HEADS, MEM_HD), lambda i, j: (i, 0, 0, 0)),
                  pl.BlockSpec((nb, n_mem, MEM_HEADS, MEM_HD), lambda i, j: (i, 0, 0, 0)),
                  const(wq_bf), const(wo_bf), const(g), const(b)],
        out_specs=pl.BlockSpec((nb, tm, D_MODEL), lambda i, j: (i, j, 0)),
        out_shape=jax.ShapeDtypeStruct((bsz, t, D_MODEL), F32),
        scratch_shapes=[pltpu.VMEM((nb * tm, D_MODEL), F32)],
        compiler_params=_params("parallel", "arbitrary"),
        name="mem_attn_ln2",
    )(x, mem_k, mem_v, wq_bf, wo_bf, g, b)


def _take_top(s, idx, n_take, tie_safe):
    rank = jnp.full(s.shape, float(n_take), F32)
    tops = []
    for i in range(n_take):
        m = jnp.max(s, axis=0, keepdims=True)
        hit = s == m
        if tie_safe:
            first = jnp.min(jnp.where(hit, idx, 1e9), axis=0, keepdims=True)
            hit = idx == first
        rank = jnp.where(hit, float(i), rank)
        s = jnp.where(hit, -jnp.inf, s)
        tops.append(m)
    return rank, tops, s


def _n_removed(s):
    return jnp.sum(jnp.where(s == -jnp.inf, 1.0, 0.0), axis=0, keepdims=True)


def _peer_route(h, lanes, qh_ref, sk_ref, lr_ref, e1_ref, rank2_ref, e2_ref, tie_safe):
    k = PEER_TOPK
    s1 = _dot_nt(sk_ref[h, 0], qh_ref[h, lanes, 0:PEER_KEY_DIM])
    s2 = _dot_nt(sk_ref[h, 1], qh_ref[h, lanes, PEER_KEY_DIM:2 * PEER_KEY_DIM])
    kidx = lax.broadcasted_iota(jnp.int32, (PEER_NKEYS, 1), 0).astype(F32)
    rank1, top1, rest1 = _take_top(s1, kidx, k, tie_safe)
    rank2, top2, rest2 = _take_top(s2, kidx, k, tie_safe)
    t2 = jnp.concatenate(top2, axis=0)
    r8 = lax.broadcasted_iota(jnp.int32, (SUBLANES, 1), 0).astype(F32)
    r16 = lax.broadcasted_iota(jnp.int32, (2 * SUBLANES, 1), 0).astype(F32)
    blocks = [top1[0] + t2]
    flat = [r16]
    off = [jnp.zeros_like(r16)]
    for i in range(1, SUBLANES):
        blocks.append(top1[i] + t2[:SUBLANES])
        flat.append(float(i * k) + r8)
        off.append(jnp.where(r8 < float(k // (i + 1)), 0.0, -jnp.inf))
    blocks.append(jnp.concatenate(top1[SUBLANES:], axis=0) + top2[0])
    flat.append(float(k) * (r8 + float(SUBLANES)))
    off.append(jnp.zeros_like(r8))
    cand = jnp.concatenate(blocks, axis=0) + jnp.concatenate(off, axis=0)
    _, ctops, rest_c = _take_top(cand, jnp.concatenate(flat, axis=0), k, tie_safe)
    cnt = jnp.where(rest_c == -jnp.inf, 1.0, 0.0) - jnp.where(cand == -jnp.inf, 1.0, 0.0)
    z = jnp.sum(cnt * jnp.exp(cand - ctops[0]), axis=0, keepdims=True)
    lrow = jnp.zeros(s1.shape, F32)
    row0 = 0
    for i in range(SUBLANES):
        n_rows = 2 * SUBLANES if i == 0 else SUBLANES
        l_i = jnp.sum(cnt[row0:row0 + n_rows], axis=0, keepdims=True)
        lrow = jnp.where(rank1 == float(i), l_i, lrow)
        row0 += n_rows
    for r in range(SUBLANES):
        lrow = jnp.where(rank1 == float(SUBLANES + r), cnt[row0 + r:row0 + r + 1], lrow)
    lr_ref[h, :, lanes] = lrow
    e1_ref[h, :, lanes] = jnp.exp(s1 - top1[0]) * (1.0 / z)
    rank2_ref[h, :, lanes] = rank2.astype(BF16)
    e2_ref[h, :, lanes] = jnp.exp(s2 - top2[0]).astype(BF16)
    bad = (jnp.abs(_n_removed(rest1) - float(k)) + jnp.abs(_n_removed(rest2) - float(k))
           + jnp.abs(jnp.sum(cnt, axis=0, keepdims=True) - float(k)))
    return jnp.max(bad) > 0.5


def _peer_kernel(x_ref, wq_ref, sk_ref, u_ref, vt_ref, g_ref, b_ref, y_ref,
                 xt_ref, qh_ref, lr_ref, e1_ref, rank2_ref, e2_ref, acc_ref):
    kb = pl.program_id(1)
    tm = x_ref.shape[0]
    eb = u_ref.shape[0]
    a_per_block = eb // PEER_NKEYS

    @pl.when(kb == 0)
    def _():
        x = x_ref[...]
        xt_ref[...] = x.T.astype(BF16)
        acc_ref[...] = jnp.zeros_like(acc_ref)
        qh =jnp.dot(x.astype(BF16), wq_ref[...], preferred_element_type=F32)
        for h in range(PEER_HEADS):
            qh_ref[h] = qh[:, 2 * h * PEER_KEY_DIM:2 * (h + 1) * PEER_KEY_DIM].astype(BF16)
        refs = (qh_ref, sk_ref, lr_ref, e1_ref, rank2_ref, e2_ref)
        lane_tiles = [slice(lt * LANES, (lt + 1) * LANES) for lt in range(tm // LANES)]
        group = PEER_ROUTE_GROUP if len(lane_tiles) % PEER_ROUTE_GROUP == 0 else 1
        for g0 in range(0, len(lane_tiles), group):
            tiles = lane_tiles[g0:g0 + group]

            def route(h, carry, tiles=tiles):
                ties = [_peer_route(h, lanes, *refs, tie_safe=False) for lanes in tiles]
                tie = ties[0]
                for t in ties[1:]:
                    tie = jnp.logical_or(tie, t)

                @pl.when(tie)
                def _():
                    for lanes in tiles:
                        _peer_route(h, lanes, *refs, tie_safe=True)
                return carry

            lax.fori_loop(0, PEER_HEADS, route, 0)

    ht = jnp.dot(u_ref[...], xt_ref[...], preferred_element_type=F32)
    gh = _gelu(ht).astype(BF16)
    wh = []
    for al in range(a_per_block):
        a = kb * a_per_block + al
        w = jnp.zeros((PEER_NKEYS, tm), BF16)
        for h in range(PEER_HEADS):
            lr = jnp.broadcast_to(lr_ref[h, pl.ds(a, 1), :].astype(BF16), (PEER_NKEYS, tm))
            e1 = jnp.broadcast_to(e1_ref[h, pl.ds(a, 1), :].astype(BF16), (PEER_NKEYS, tm))
            w = w + jnp.where(rank2_ref[h] < lr, e2_ref[h], jnp.zeros_like(w)) * e1
        wh.append(w * gh[al * PEER_NKEYS:(al + 1) * PEER_NKEYS])
    acc_ref[...] += jnp.dot(vt_ref[...], jnp.concatenate(wh, axis=0), preferred_element_type=F32)

    @pl.when(kb == pl.num_programs(1) - 1)
    def _():
        y_ref[...] = _layer_norm(DN_ALPHA * x_ref[...] + acc_ref[...].T, g_ref[...], b_ref[...])


def _peer(x, wq_bf, sk_bf, u_bf, vt_bf, g, b):
    n_tokens = x.shape[0]
    n = -(-n_tokens // LANES) * LANES
    x = jnp.pad(x, ((0, n - n_tokens), (0, 0)))
    tm = ROW_TILE if n % ROW_TILE == 0 else LANES
    n_exp = u_bf.shape[0]
    eb = PEER_EXPERT_BLOCK
    once = pl.Buffered(1)
    const2 = lambda a: pl.BlockSpec(a.shape, lambda i, k: (0, 0), pipeline_mode=once)
    return pl.pallas_call(
        _peer_kernel,
        grid=(n // tm, n_exp // eb),
        in_specs=[pl.BlockSpec((tm, D_MODEL), lambda i, k: (i, 0)),
                  const2(wq_bf),
                  pl.BlockSpec(sk_bf.shape, lambda i, k: (0, 0, 0, 0), pipeline_mode=once),
                  pl.BlockSpec((eb, D_MODEL), lambda i, k: (k, 0)),
                  pl.BlockSpec((D_MODEL, eb), lambda i, k: (0, k)),
                  const2(g), const2(b)],
        out_specs=pl.BlockSpec((tm, D_MODEL), lambda i, k: (i, 0)),
        out_shape=jax.ShapeDtypeStruct((n, D_MODEL), F32),
        scratch_shapes=[pltpu.VMEM((D_MODEL, tm), BF16),
                        pltpu.VMEM((PEER_HEADS, tm, 2 * PEER_KEY_DIM), BF16),
                        pltpu.VMEM((PEER_HEADS, PEER_NKEYS, tm), F32),
                        pltpu.VMEM((PEER_HEADS, PEER_NKEYS, tm), F32),
                        pltpu.VMEM((PEER_HEADS, PEER_NKEYS, tm), BF16),
                        pltpu.VMEM((PEER_HEADS, PEER_NKEYS, tm), BF16),
                        pltpu.VMEM((D_MODEL, tm), F32)],
        compiler_params=_params("parallel", "arbitrary"),
        name="peer_ln3",
    )(x, wq_bf, sk_bf, u_bf, vt_bf, g, b)[:n_tokens]


def kernel(x_prompt, x_sample, cache_k_cmp, cache_v_cmp, cache_k_slc, cache_v_slc, cache_k_win, cache_v_win,
           state_hgrn, cache_mem_k, cache_mem_v, page_table, mem_prompt, w_in, hgrn_lb_logits, hgrn_norm_g,
           cmp_k_w1, cmp_k_b1, cmp_k_w2, cmp_k_b2, cmp_v_w1, cmp_v_b1, cmp_v_w2, cmp_v_b2, w_out, ln1_g, ln1_b,
           w_mem_q, w_mem_k, w_mem_v, w_mem_o, ln2_g, ln2_b, peer_w_q, peer_sub_keys, peer_u, peer_v,
           ln3_g, ln3_b):
    bsz, seq, _ = x_prompt.shape
    db, dt, _ = x_sample.shape
    dtp = -(-dt // SUBLANES) * SUBLANES
    layer = 0

    w_in_pad = jnp.pad(w_in[layer], ((0, 0), (0, sum(IN_GROUP_WIDTHS) - IN_COLS))).astype(BF16)
    kwts = _cmp_weights(cmp_k_w1[layer], cmp_k_b1[layer], cmp_k_w2[layer], cmp_k_b2[layer])
    vwts = _cmp_weights(cmp_v_w1[layer], cmp_v_b1[layer], cmp_v_w2[layer], cmp_v_b2[layer])
    w_out_bf = w_out[layer].astype(BF16)
    wmq, wmk, wmv, wmo = (w[layer].astype(BF16) for w in (w_mem_q, w_mem_k, w_mem_v, w_mem_o))
    pwq = peer_w_q[layer].astype(BF16)
    psk = peer_sub_keys[layer].astype(BF16)
    pu = peer_u[layer].astype(BF16)
    pvt = peer_v[layer].astype(BF16).T
    vec = lambda a: a[layer].reshape(1, D_MODEL)
    g1, b1, g2, b2, g3, b3 = (vec(a) for a in (ln1_g, ln1_b, ln2_g, ln2_b, ln3_g, ln3_b))
    norm_g = hgrn_norm_g[layer]

    n_p = bsz * seq
    xp = x_prompt.reshape(n_p, D_MODEL)
    zh, nq, kc, vc, ks, vs, kw, vw, ngate = _inproj(xp, w_in_pad)
    per_b = lambda a: a.reshape(bsz, seq, a.shape[-1])
    chunk = min(HG_CHUNK, seq)
    o_h, s_p = _hgrn(per_b(zh), hgrn_lb_logits, norm_g, jnp.zeros((bsz, HG_HEADS, HG_D, HG_D), F32),
                     chunk=chunk, step_tokens=min(HG_STEP_TOKENS, seq), t_valid=chunk)
    kblk, vblk = _cmp_prompt(per_b(kc), per_b(vc), kwts, vwts)
    o_n = _nsa_prompt(per_b(nq), per_b(ngate), kblk, vblk, per_b(ks), per_b(vs), per_b(kw), per_b(vw))
    x1 = _outproj(o_h.reshape(n_p, HG_WIDTH), o_n.reshape(n_p, NSA_WIDTH), xp, w_out_bf, g1, b1)
    n_mem = mem_prompt.shape[1]
    mem_k, mem_v = _memkv(mem_prompt.reshape(bsz * n_mem, D_MODEL), wmk, wmv)
    mem_k = mem_k.reshape(bsz, n_mem, MEM_HEADS, MEM_HD)
    mem_v = mem_v.reshape(bsz, n_mem, MEM_HEADS, MEM_HD)
    x2 = _memattn(x1.reshape(bsz, seq, D_MODEL), mem_k, mem_v, wmq, wmo, g2, b2, nb=1, tm=min(ROW_TILE, seq))
    y_p = _peer(x2.reshape(n_p, D_MODEL), pwq, psk, pu, pvt, g3, b3).reshape(bsz, seq, D_MODEL)

    kv5 = lambda a: a.reshape(1, bsz, seq, NSA_GROUPS, NSA_HD)
    n_win = min(WINDOW, seq)
    win5 = lambda a: a.reshape(bsz, seq, NSA_GROUPS, NSA_HD)[None, :, seq - n_win:]
    mem5 = lambda a: a[None]

    n_s = db * dtp
    xs = jnp.pad(x_sample, ((0, 0), (0, dtp - dt), (0, 0))).reshape(n_s, D_MODEL)
    zh, nq, kc_s, vc_s, ks_s, vs_s, kw_s, vw_s, ngate = _inproj(xs, w_in_pad)
    per_s = lambda a: a.reshape(db, dtp, a.shape[-1])
    o_h, s_s = _hgrn(per_s(zh), hgrn_lb_logits, norm_g, state_hgrn[layer],
                     chunk=dtp, step_tokens=dtp, t_valid=dt)
    o_n = _nsa_sample(page_table, per_s(nq), per_s(ngate), per_s(ks_s), per_s(vs_s), per_s(kw_s), per_s(vw_s),
                      jnp.transpose(cache_k_win[layer], (0, 2, 3, 1)), jnp.transpose(cache_v_win[layer], (0, 2, 3, 1)),
                      kwts, vwts, cache_k_cmp[layer], cache_v_cmp[layer], cache_k_slc[layer], cache_v_slc[layer],
                      t_valid=dt)
    x1 = _outproj(o_h.reshape(n_s, HG_WIDTH), o_n.reshape(n_s, NSA_WIDTH), xs, w_out_bf, g1, b1)
    nb = min(4, db)
    x2 = _memattn(x1.reshape(db, dtp, D_MODEL), cache_mem_k[layer], cache_mem_v[layer],
                  wmq, wmo, g2, b2, nb=nb, tm=dtp)
    y_s = _peer(x2.reshape(n_s, D_MODEL), pwq, psk, pu, pvt, g3, b3).reshape(db, dtp, D_MODEL)[:, :dt]
    skv5 = lambda a: a.reshape(db, dtp, NSA_GROUPS, NSA_HD)[None, :, :dt]

    return (y_p, y_s, kv5(kc), kv5(vc), kv5(ks), kv5(vs), win5(kw), win5(vw), s_p[None], mem5(mem_k), mem5(mem_v),
            skv5(kc_s), skv5(vc_s), skv5(ks_s), skv5(vs_s), skv5(kw_s), skv5(vw_s), s_s[None])
```

```python
import functools

import numpy as np
import jax
import jax.numpy as jnp
from jax import lax
from jax.experimental import pallas as pl
from jax.experimental.pallas import tpu as pltpu

F32 = jnp.float32
BF16 = jnp.bfloat16

D_MODEL = 1024
HG_HEADS = 4
HG_D = 128
HG_WIDTH = HG_HEADS * HG_D
NSA_HEADS = 8
NSA_GROUPS = 2
NSA_HPG = NSA_HEADS // NSA_GROUPS
NSA_HD = 64
NSA_WIDTH = NSA_HEADS * NSA_HD
KV_WIDTH = NSA_GROUPS * NSA_HD
CMP_BLOCK = 32
CMP_STRIDE = 16
CMP_HIDDEN = 128
SEL_BLOCK = 64
N_SEL = 16
WINDOW = 512
PAGE_SIZE = 128
MEM_HEADS = 4
MEM_HD = D_MODEL // MEM_HEADS
PEER_HEADS = 8
PEER_NKEYS = 128
PEER_KEY_DIM = 128
PEER_TOPK = 16
DN_ALPHA = 2.0 ** 0.25
LN_EPS = 1e-5
NEG_INF = -1e30
LOG2E = 1.4426950408889634
ALIBI_SLOPES = tuple(2.0 ** (-8.0 * (h + 1) / NSA_HEADS) for h in range(NSA_HEADS))

LANES = 128
SUBLANES = 8
VMEM_LIMIT_BYTES = 56 * 1024 * 1024

HG_CHUNK = 32
HG_STEP_TOKENS = 256
HG_SEQS_PER_STEP = 2
NSA_Q_TILE = 128
ROW_TILE = 512
PEER_EXPERT_BLOCK = 1024
PEER_ROUTE_GROUP = 4
SLC_KEY_CHUNK = 512

IN_GROUP_WIDTHS = (4 * HG_WIDTH, NSA_WIDTH) + (KV_WIDTH,) * 6 + (LANES,)
IN_COLS = 4 * HG_WIDTH + NSA_WIDTH + 6 * KV_WIDTH + 3 * NSA_HEADS


def _params(*sem):
    return pltpu.CompilerParams(dimension_semantics=sem, vmem_limit_bytes=VMEM_LIMIT_BYTES)


def _gelu(x):
    return 0.5 * x * (1.0 + jnp.tanh(0.7978845608028654 * (x + 0.044715 * (x * x * x))))


def _layer_norm(y, g, b):
    mu = jnp.mean(y, -1, keepdims=True)
    yc = y - mu
    var = jnp.mean(yc * yc, -1, keepdims=True)
    return yc * lax.rsqrt(var + LN_EPS) * g + b


def _dot(a, b):
    return jnp.dot(a.astype(BF16), b.astype(BF16), preferred_element_type=F32)


def _dot_nt(a, b):
    return lax.dot_general(a.astype(BF16), b.astype(BF16), (((1,), (1,)), ((), ())),
                           preferred_element_type=F32)


def _dot_tn(a, b):
    return lax.dot_general(a.astype(BF16), b.astype(BF16), (((0,), (0,)), ((), ())),
                           preferred_element_type=F32)


def _masked_exp(s, mask):
    sm = jnp.where(mask, s, NEG_INF)
    m = jnp.max(sm, -1, keepdims=True)
    e = jnp.where(mask, jnp.exp(sm - m), 0.0)
    return e, jnp.sum(e, -1, keepdims=True)


def _safe_inv(d):
    return jnp.where(d > 0.0, 1.0 / jnp.where(d > 0.0, d, 1.0), 0.0)


def _inproj_kernel(x_ref, w_ref, *out_refs):
    xb = x_ref[...].astype(BF16)
    off = 0
    for o_ref in out_refs:
        width = o_ref.shape[-1]
        o_ref[...] = jnp.dot(xb, w_ref[:, off:off + width], preferred_element_type=F32)
        off += width


def _inproj(x, w_pad):
    n = x.shape[0]
    tm = min(ROW_TILE, n)
    return pl.pallas_call(
        _inproj_kernel,
        grid=(n // tm,),
        in_specs=[pl.BlockSpec((tm, D_MODEL), lambda i: (i, 0)),
                  pl.BlockSpec(w_pad.shape, lambda i: (0, 0))],
        out_specs=[pl.BlockSpec((tm, w), lambda i: (i, 0)) for w in IN_GROUP_WIDTHS],
        out_shape=[jax.ShapeDtypeStruct((n, w), F32) for w in IN_GROUP_WIDTHS],
        compiler_params=_params("parallel"),
        name="inproj",
    )(x, w_pad)


def _inproj_prompt_kernel(x_ref, w_ref, wkv_t_ref, zh_ref, nq_ref, kc_ref, vc_ref, ng_ref, *kv_t_refs):
    xb = x_ref[...].astype(BF16)
    off = 0
    for o_ref in (zh_ref, nq_ref, kc_ref, vc_ref, ng_ref):
        width = o_ref.shape[-1]
        o_ref[...] = jnp.dot(xb, w_ref[:, off:off + width], preferred_element_type=F32)
        off += width
    z_t = lax.dot_general(wkv_t_ref[...], xb, (((1,), (1,)), ((), ())), preferred_element_type=F32)
    for i, o_ref in enumerate(kv_t_refs):
        o_ref[0] = z_t[i * KV_WIDTH:(i + 1) * KV_WIDTH]


def _inproj_prompt(x, w_main, wkv_t, bsz, seq):
    n = x.shape[0]
    tm = min(ROW_TILE, seq)
    tiles = seq // tm
    widths = (4 * HG_WIDTH, NSA_WIDTH, KV_WIDTH, KV_WIDTH, LANES)
    n_kv = wkv_t.shape[0] // KV_WIDTH
    return pl.pallas_call(
        _inproj_prompt_kernel,
        grid=(n // tm,),
        in_specs=[pl.BlockSpec((tm, D_MODEL), lambda i: (i, 0)),
                  pl.BlockSpec(w_main.shape, lambda i: (0, 0)),
                  pl.BlockSpec(wkv_t.shape, lambda i: (0, 0))],
        out_specs=[pl.BlockSpec((tm, w), lambda i: (i, 0)) for w in widths]
                  + [pl.BlockSpec((1, KV_WIDTH, tm), lambda i: (i // tiles, 0, i % tiles))] * n_kv,
        out_shape=[jax.ShapeDtypeStruct((n, w), F32) for w in widths]
                  + [jax.ShapeDtypeStruct((bsz, KV_WIDTH, seq), F32)] * n_kv,
        compiler_params=_params("parallel"),
        name="inproj_prompt",
    )(x, w_main, wkv_t)


def _hgrn_kernel(zh_ref, lbl_ref, ng_ref, s0_ref, tri_ref, o_ref, sout_ref, st_ref, obuf_ref,
                 *, chunk, t_valid):
    ti = pl.program_id(1)
    n_seq, step_tokens, _ = zh_ref.shape
    units = [(s, h) for s in range(n_seq) for h in range(HG_HEADS)]

    @pl.when(ti == 0)
    def _():
        for s, h in units:
            st_ref[s, h] = s0_ref[s, h].T

    logits = lbl_ref[...]
    le = jnp.exp(logits - jnp.max(logits, axis=0, keepdims=True))
    lb_all = le[0:1] / jnp.sum(le, axis=0, keepdims=True)
    norm_g = ng_ref[...]
    row = lax.broadcasted_iota(jnp.int32, (chunk, 1), 0)
    row8 = lax.broadcasted_iota(jnp.int32, (SUBLANES, 1), 0)
    n_tiles = chunk // SUBLANES
    tile_match = (lax.broadcasted_iota(jnp.int32, (chunk, n_tiles * chunk), 0) // SUBLANES
                  == lax.broadcasted_iota(jnp.int32, (chunk, n_tiles * chunk), 1) // chunk)

    def do_chunk(c, carry):
        r0 = pl.multiple_of(c * chunk, chunk)
        rows = pl.ds(r0, chunk)
        for s, h in units:
            cs = h * HG_D
            q = zh_ref[s, rows, cs:cs + HG_D]
            zf = zh_ref[s, rows, HG_WIDTH + cs:HG_WIDTH + cs + HG_D]
            v = zh_ref[s, rows, 2 * HG_WIDTH + cs:2 * HG_WIDTH + cs + HG_D]
            zg = zh_ref[s, rows, 3 * HG_WIDTH + cs:3 * HG_WIDTH + cs + HG_D]
            lb = lb_all[:, cs:cs + HG_D]
            lf = jnp.log(lb + (1.0 - lb) * jax.nn.sigmoid(zf))
            k = (1.0 - lb) * jax.nn.sigmoid(-zf)
            if t_valid < chunk:
                lf = jnp.where(row < t_valid, lf, 0.0)
                k = jnp.where(row < t_valid, k, 0.0)
            if chunk <= SUBLANES:
                g = jnp.where(row >= 0, lf[0:1], 0.0)
                for t in range(1, chunk):
                    g = g + jnp.where(row >= t, lf[t:t + 1], 0.0)
            else:
                g = jnp.dot(tri_ref[...], lf, precision=lax.Precision.HIGHEST, preferred_element_type=F32)
            st = st_ref[s, h]
            o = _dot_nt(q * jnp.exp(g), st)
            for t in range(chunk):
                r0 = t // SUBLANES * SUBLANES
                rel = jnp.where(row8 <= t - r0, g[t:t + 1] - g[r0:r0 + SUBLANES], NEG_INF)
                p = (q[t:t + 1] * k[r0:r0 + SUBLANES]) * jnp.exp(rel)
                w = jnp.sum(p, axis=-1, keepdims=True)
                obuf_ref[s, h, t:t + 1, :] = jnp.sum(w * v[r0:r0 + SUBLANES], axis=0, keepdims=True)
            o = o + obuf_ref[s, h]
            if n_tiles > 1:
                g_ref_rows = jnp.concatenate(
                    [jnp.broadcast_to(g[max(i * SUBLANES - 1, 0):max(i * SUBLANES - 1, 0) + 1], (SUBLANES, HG_D))
                     for i in range(n_tiles)], axis=0)
                qa = q * jnp.exp(g - g_ref_rows)
                ka = [jnp.zeros_like(k)]
                for i in range(1, n_tiles):
                    shift = jnp.where(row < i * SUBLANES, g[i * SUBLANES - 1:i * SUBLANES] - g, NEG_INF)
                    ka.append(k * jnp.exp(shift))
                att = _dot_nt(qa, jnp.concatenate(ka, axis=0))
                att = jnp.where(tile_match, att, 0.0)
                o = o + _dot(att, jnp.concatenate([v] * n_tiles, axis=0))
            g_end = g[chunk - 1:chunk]
            kk = k * jnp.exp(g_end - g)
            st_ref[s, h] = st * jnp.exp(g_end) + _dot_tn(v, kk)
            o = o * lax.rsqrt(jnp.mean(o * o, -1, keepdims=True) + LN_EPS) * norm_g
            o_ref[s, rows, cs:cs + HG_D] = o * jax.nn.sigmoid(zg)
        return carry

    lax.fori_loop(0, step_tokens // chunk, do_chunk, 0)

    @pl.when(ti == pl.num_programs(1) - 1)
    def _():
        for s, h in units:
            sout_ref[s, h] = st_ref[s, h].T


def _hgrn(zh, lb_logits, norm_g, s0, *, chunk, step_tokens, t_valid):
    b, t, _ = zh.shape
    n_seq = HG_SEQS_PER_STEP if b % HG_SEQS_PER_STEP == 0 else 1
    tri = jnp.asarray(np.tril(np.ones((chunk, chunk), np.float32)))
    kern = functools.partial(_hgrn_kernel, chunk=chunk, t_valid=t_valid)
    return pl.pallas_call(
        kern,
        grid=(b // n_seq, t // step_tokens),
        in_specs=[pl.BlockSpec((n_seq, step_tokens, 4 * HG_WIDTH), lambda i, j: (i, j, 0)),
                  pl.BlockSpec(lb_logits.shape, lambda i, j: (0, 0)),
                  pl.BlockSpec((1, HG_D), lambda i, j: (0, 0)),
                  pl.BlockSpec((n_seq, HG_HEADS, HG_D, HG_D), lambda i, j: (i, 0, 0, 0)),
                  pl.BlockSpec((chunk, chunk), lambda i, j: (0, 0))],
        out_specs=[pl.BlockSpec((n_seq, step_tokens, HG_WIDTH), lambda i, j: (i, j, 0)),
                   pl.BlockSpec((n_seq, HG_HEADS, HG_D, HG_D), lambda i, j: (i, 0, 0, 0))],
        out_shape=[jax.ShapeDtypeStruct((b, t, HG_WIDTH), F32),
                   jax.ShapeDtypeStruct((b, HG_HEADS, HG_D, HG_D), F32)],
        scratch_shapes=[pltpu.VMEM((n_seq, HG_HEADS, HG_D, HG_D), F32),
                        pltpu.VMEM((n_seq, HG_HEADS, chunk, HG_D), F32)],
        compiler_params=_params("parallel", "arbitrary"),
        name="hgrn",
    )(zh, lb_logits, norm_g.reshape(1, HG_D), s0, tri)


def _cmp_blocks(view, w1_ref, b1_ref, w2_ref, b2_ref):
    return _cmp_finish(_dot(view, w1_ref[...]), b1_ref, w2_ref, b2_ref)


def _cmp_blocks_from_rows(rows_ref, w1_ref, b1_ref, w2_ref, b2_ref):
    n_sub = rows_ref.shape[0] // CMP_STRIDE
    a = None
    for p in range(CMP_STRIDE):
        part = _dot(rows_ref[pl.ds(p, n_sub, stride=CMP_STRIDE), :], w1_ref[p * KV_WIDTH:(p + 1) * KV_WIDTH, :])
        a = part if a is None else a + part
    return _cmp_finish(a, b1_ref, w2_ref, b2_ref)


def _cmp_finish(a, b1_ref, w2_ref, b2_ref):
    n_sub = a.shape[0]
    hid = []
    for g in range(NSA_GROUPS):
        a0 = a[:, (2 * g) * CMP_HIDDEN:(2 * g + 1) * CMP_HIDDEN]
        a1 = a[:, (2 * g + 1) * CMP_HIDDEN:(2 * g + 2) * CMP_HIDDEN]
        hid.append(a0 + pltpu.roll(a1, n_sub - 1, axis=0))
    h = jnp.concatenate(hid, axis=-1) + b1_ref[...]
    return _dot(_gelu(h), w2_ref[...]) + b2_ref[...]


def _cmp_kernel(kv_ref, vv_ref, kw1, kb1, kw2, kb2, vw1, vb1, vw2, vb2, ko_ref, vo_ref):
    ko_ref[0] = _cmp_blocks(kv_ref[0], kw1, kb1, kw2, kb2)
    vo_ref[0] = _cmp_blocks(vv_ref[0], vw1, vb1, vw2, vb2)


def _cmp_weights(w1, b1, w2, b2):
    w1r = w1.reshape(CMP_BLOCK // CMP_STRIDE, CMP_STRIDE, NSA_HD, CMP_HIDDEN)
    big = jnp.zeros((CMP_STRIDE, NSA_GROUPS, NSA_HD, NSA_GROUPS, 2, CMP_HIDDEN), F32)
    for g in range(NSA_GROUPS):
        big = big.at[:, g, :, g, :, :].set(jnp.transpose(w1r, (1, 2, 0, 3)))
    w1b = big.reshape(CMP_STRIDE * KV_WIDTH, NSA_GROUPS * 2 * CMP_HIDDEN).astype(BF16)
    w2b = jnp.zeros((NSA_GROUPS, CMP_HIDDEN, NSA_GROUPS, NSA_HD), F32)
    for g in range(NSA_GROUPS):
        w2b = w2b.at[g, :, g, :].set(w2)
    w2b = w2b.reshape(NSA_GROUPS * CMP_HIDDEN, KV_WIDTH).astype(BF16)
    b1b = jnp.tile(b1, NSA_GROUPS).reshape(1, NSA_GROUPS * CMP_HIDDEN)
    b2b = jnp.tile(b2, NSA_GROUPS).reshape(1, KV_WIDTH)
    return w1b, b1b, w2b, b2b


def _cmp_prompt(kc, vc, kwts, vwts):
    b, t, _ = kc.shape
    n_sub = t // CMP_STRIDE
    view = lambda a: a.reshape(b, n_sub, CMP_STRIDE * KV_WIDTH)
    wspecs = [pl.BlockSpec(w.shape, lambda i: (0, 0)) for w in kwts + vwts]
    return pl.pallas_call(
        _cmp_kernel,
        grid=(b,),
        in_specs=[pl.BlockSpec((1, n_sub, CMP_STRIDE * KV_WIDTH), lambda i: (i, 0, 0))] * 2 + wspecs,
        out_specs=[pl.BlockSpec((1, n_sub, KV_WIDTH), lambda i: (i, 0, 0))] * 2,
        out_shape=[jax.ShapeDtypeStruct((b, n_sub, KV_WIDTH), F32)] * 2,
        compiler_params=_params("parallel"),
        name="cmp_prompt",
    )(view(kc), view(vc), *kwts, *vwts)


def _cmp_to_slc(n_cmp, n_slc):
    cs = np.arange(n_cmp)[:, None] * CMP_STRIDE
    ss = np.arange(n_slc)[None, :] * SEL_BLOCK
    shared = np.clip(np.minimum(cs + CMP_BLOCK, ss + SEL_BLOCK) - np.maximum(cs, ss), 0, None)
    return (shared / CMP_STRIDE).astype(np.float32)


def _softmax2(s):
    m = jnp.max(s, -1, keepdims=True)
    e = jnp.exp2(s - m)
    inv = jnp.where(m > 0.5 * NEG_INF, 1.0 / jnp.sum(e, -1, keepdims=True), 0.0)
    return e, inv


def _nsa_prompt_kernel(q_ref, gate_ref, kb_ref, vb_ref, ks_ref, vs_ref, kw_ref, vw_ref, mt_ref, ex_ref,
                       o_ref, bias_ref, *, n_cmp, n_var):
    tq = q_ref.shape[1]
    t_all = ks_ref.shape[2]
    n_sub = kb_ref.shape[1]
    n_slc = mt_ref.shape[0]
    tile = pl.program_id(1)
    t0 = tile * tq
    q = q_ref[0] * (NSA_HD ** -0.5 * LOG2E)
    slopes = tuple(s * LOG2E for s in ALIBI_SLOPES)
    gates = jax.nn.sigmoid(gate_ref[0])
    qpos = t0 + lax.broadcasted_iota(jnp.int32, (tq, 1), 0)

    def stack_heads(g):
        return jnp.concatenate([q[:, h * NSA_HD:(h + 1) * NSA_HD] for h in range(g * NSA_HPG, (g + 1) * NSA_HPG)],
                               axis=0)

    cidx = lax.broadcasted_iota(jnp.int32, (1, n_sub), 1)
    kend =cidx * CMP_STRIDE + CMP_BLOCK - 1
    bias_c = jnp.where((qpos - kend >= 0) & (cidx < n_cmp), 0.0, NEG_INF)
    kend_rel = (kend - t0).astype(F32)
    nidx = lax.broadcasted_iota(jnp.int32, (n_slc, 1), 0)
    tpos = t0 + lax.broadcasted_iota(jnp.int32, (1, tq), 1)
    elig = nidx * SEL_BLOCK <= tpos
    force = (nidx == tpos // SEL_BLOCK) | (nidx == 0)
    causal = qpos - lax.broadcasted_iota(jnp.int32, (1, t_all), 1) >= 0
    span = min(WINDOW + tq, t_all)
    start = pl.multiple_of(jnp.clip(t0 - WINDOW, 0, t_all - span), tq)
    kpos_w = start + lax.broadcasted_iota(jnp.int32, (1, span), 1)
    bias_w = jnp.where((qpos - kpos_w >= 0) & (qpos - kpos_w < WINDOW), 0.0, NEG_INF)
    kpos_w_rel = (kpos_w - t0).astype(F32)

    for g in range(NSA_GROUPS):
        gs = g * NSA_HD
        kb = kb_ref[0, :, gs:gs + NSA_HD]
        vb = vb_ref[0, :, gs:gs + NSA_HD]
        kw = kw_ref[0, gs:gs + NSA_HD, pl.ds(start, span)]
        vw = vw_ref[0, gs:gs + NSA_HD, pl.ds(start, span)]
        heads = range(g * NSA_HPG, (g + 1) * NSA_HPG)
        qg = stack_heads(g)
        e, inv = _softmax2(_dot_nt(qg, kb) + jnp.concatenate([bias_c + slopes[h] * kend_rel for h in heads], axis=0))
        p = e * inv
        o_cmp = _dot(p, vb)
        psum = p[0:tq]
        for j in range(1, NSA_HPG):
            psum = psum + p[j * tq:(j + 1) * tq]
        e, inv = _softmax2(_dot(qg, kw) + jnp.concatenate([bias_w + slopes[h] * kpos_w_rel for h in heads], axis=0))
        o_win = _dot_nt(e, vw) * inv
        for j, h in enumerate(heads):
            rs = slice(j * tq, (j + 1) * tq)
            o_ref[0, :, h * NSA_HD:(h + 1) * NSA_HD] = (
                gates[:, 3 * h:3 * h + 1] * o_cmp[rs] + gates[:, 3 * h + 2:3 * h + 3] * o_win[rs])
        imp_t = lax.dot_general(mt_ref[...], psum, (((1,), (1,)), ((), ())),
                                precision=lax.Precision.HIGHEST, preferred_element_type=F32)
        val = jnp.where(force, jnp.inf, jnp.where(elig, imp_t, -jnp.inf))
        cnt = jnp.zeros((n_slc, tq), F32)
        for m in range(n_slc):
            other = val[m:m + 1]
            tie = jnp.where(nidx > m, 1.0, 0.0)
            cnt = cnt + jnp.where(other > val, 1.0, jnp.where(other == val, tie, 0.0))
        sel_t = jnp.where(cnt < float(min(N_SEL, n_slc)), 1.0, 0.0)
        sel_keys = _dot(sel_t.T, ex_ref[...])
        bias_ref[g] = jnp.where((sel_keys > 0.5) & causal, 0.0, NEG_INF)

    tiles_per_var = (t_all // tq) // n_var
    for v in range(n_var):
        n_keys = (v + 1) * tiles_per_var * tq

        @pl.when(tile // tiles_per_var == v)
        def _(n_keys=n_keys):
            kpos_rel = (lax.broadcasted_iota(jnp.int32, (1, n_keys), 1) - t0).astype(F32)
            for g in range(NSA_GROUPS):
                gs = g * NSA_HD
                ks = ks_ref[0, gs:gs + NSA_HD, 0:n_keys]
                vs = vs_ref[0, gs:gs + NSA_HD, 0:n_keys]
                bias_s = bias_ref[g, :, 0:n_keys]
                heads = range(g * NSA_HPG, (g + 1) * NSA_HPG)
                bias = jnp.concatenate([bias_s + slopes[h] * kpos_rel for h in heads], axis=0)
                e, inv = _softmax2(_dot(stack_heads(g), ks) + bias)
                o_slc = _dot_nt(e, vs) * inv
                for j, h in enumerate(heads):
                    o_ref[0, :, h * NSA_HD:(h + 1) * NSA_HD] += gates[:, 3 * h + 1:3 * h + 2] * o_slc[j * tq:(j + 1) * tq]


def _nsa_prompt(nq, ng, kblk, vblk, ks, vs, kw, vw):
    b, t, _ = nq.shape
    tq = min(NSA_Q_TILE, t)
    n_sub = kblk.shape[1]
    n_cmp = n_sub - CMP_BLOCK // CMP_STRIDE + 1
    n_slc = -(-t // SEL_BLOCK)
    n_var = max(1, t // SLC_KEY_CHUNK)
    assert (t // tq) % n_var == 0
    mt = np.zeros((n_slc, n_sub), np.float32)
    mt[:, :n_cmp] = _cmp_to_slc(n_cmp, n_slc).T
    expand = (np.arange(t)[None, :] // SEL_BLOCK == np.arange(n_slc)[:, None]).astype(np.float32)
    full = lambda w: pl.BlockSpec((1, w, t), lambda i, j: (i, 0, 0))
    return pl.pallas_call(
        functools.partial(_nsa_prompt_kernel, n_cmp=n_cmp, n_var=n_var),
        grid=(b, t // tq),
        in_specs=[pl.BlockSpec((1, tq, NSA_WIDTH), lambda i, j: (i, j, 0)),
                  pl.BlockSpec((1, tq, LANES), lambda i, j: (i, j, 0)),
                  pl.BlockSpec((1, n_sub, KV_WIDTH), lambda i, j: (i, 0, 0)),
                  pl.BlockSpec((1, n_sub, KV_WIDTH), lambda i, j: (i, 0, 0)),
                  full(KV_WIDTH), full(KV_WIDTH), full(KV_WIDTH), full(KV_WIDTH),
                  pl.BlockSpec(mt.shape, lambda i, j: (0, 0)),
                  pl.BlockSpec(expand.shape, lambda i, j: (0, 0))],
        out_specs=pl.BlockSpec((1, tq, NSA_WIDTH), lambda i, j: (i, j, 0)),
        out_shape=jax.ShapeDtypeStruct((b, t, NSA_WIDTH), F32),
        scratch_shapes=[pltpu.VMEM((NSA_GROUPS, tq, t), F32)],
        compiler_params=_params("parallel", "arbitrary"),
        name="nsa_prompt",
    )(nq, ng, kblk, vblk, ks, vs, kw, vw, jnp.asarray(mt), jnp.asarray(expand, dtype=BF16))


def _nsa_sample_kernel(pt_ref, q_ref, gate_ref, ksn_ref, vsn_ref, kwn_ref, vwn_ref, wk_ref, wv_ref,
                       kw1, kb1, kw2, kb2, vw1, vb1, vw2, vb2, m_ref, ex_ref,
                       kc_pool, vc_pool, ks_pool, vs_pool,
                       o_ref, kc_buf, vc_buf, ks_buf, vs_buf, sem, *, t_valid, n_slc):
    b = pl.program_id(0)
    n_pages = pt_ref.shape[1]
    tp = q_ref.shape[1]
    past = n_pages * PAGE_SIZE
    n_sub = past // CMP_STRIDE
    n_cmp = n_sub - CMP_BLOCK // CMP_STRIDE + 1
    n_past_blocks = past // SEL_BLOCK

    pools = ((kc_pool, kc_buf, False), (vc_pool, vc_buf, False), (ks_pool, ks_buf, True), (vs_pool, vs_buf, True))

    def page_copy(i, j):
        pool, buf, rows_on_lanes = pools[i]
        rows = pl.ds(pl.multiple_of(j * PAGE_SIZE, PAGE_SIZE), PAGE_SIZE)
        dst = buf.at[:, :, rows] if rows_on_lanes else buf.at[rows]
        return pltpu.make_async_copy(pool.at[pt_ref[b, j]], dst, sem.at[i])

    def start_pages(which):
        def body(j, carry):
            for i in which:
                page_copy(i, j).start()
            return carry
        lax.fori_loop(0, n_pages, body, 0)

    def wait_pages(which):
        def body(j, carry):
            for i in which:
                page_copy(i, j).wait()
            return carry
        lax.fori_loop(0, n_pages, body, 0)

    start_pages((0, 1))
    start_pages((2, 3))
    wait_pages((0, 1))
    kblk = _cmp_blocks_from_rows(kc_buf, kw1, kb1, kw2, kb2)
    vblk = _cmp_blocks_from_rows(vc_buf, vw1, vb1, vw2, vb2)
    wait_pages((2, 3))

    q = q_ref[0] * (NSA_HD ** -0.5)
    gates = jax.nn.sigmoid(gate_ref[0])
    trow = lax.broadcasted_iota(jnp.int32, (tp, 1), 0)
    trow_q = jnp.concatenate([trow] * NSA_HPG, axis=0)
    qpos_q = past + trow_q
    cidx = lax.broadcasted_iota(jnp.int32, (1, n_sub), 1)
    dist_c = qpos_q - (cidx * CMP_STRIDE + CMP_BLOCK - 1)
    mask_c = (dist_c >= 0) & (cidx < n_cmp)
    dist_cf = dist_c.astype(F32)
    n_rows = m_ref.shape[0]
    nidx = lax.broadcasted_iota(jnp.int32, (n_rows, 1), 0)
    nidx_f = nidx.astype(F32)
    qpos_t = past + lax.broadcasted_iota(jnp.int32, (1, tp), 1)
    elig = (nidx * SEL_BLOCK <= qpos_t) & (nidx < n_slc)
    force = ((nidx == qpos_t // SEL_BLOCK) | (nidx == 0)) & (nidx < n_slc)
    eye = lax.broadcasted_iota(jnp.int32, (tp, tp), 0) == lax.broadcasted_iota(jnp.int32, (tp, tp), 1)
    kpos = lax.broadcasted_iota(jnp.int32, (1, past), 1)
    dist_pf = (qpos_q - kpos).astype(F32)
    rnew = lax.broadcasted_iota(jnp.int32, (1, tp), 1)
    dist_n = trow_q - rnew
    mask_n = (dist_n >= 0) & (rnew < t_valid)
    dist_nf = dist_n.astype(F32)
    n_buf = wk_ref.shape[3]
    dist_b = qpos_q - (past - n_buf + lax.broadcasted_iota(jnp.int32, (1, n_buf), 1))
    mask_b = (dist_b >= 0) & (dist_b < WINDOW)
    dist_bf = dist_b.astype(F32)

    for g in range(NSA_GROUPS):
        gs = g * NSA_HD
        heads = range(g * NSA_HPG, (g + 1) * NSA_HPG)
        qg = jnp.concatenate([q[:, h * NSA_HD:(h + 1) * NSA_HD] for h in heads], axis=0)
        slope = jnp.concatenate([jnp.full((tp, 1), ALIBI_SLOPES[h], F32) for h in heads], axis=0)
        s = _dot_nt(qg, kblk[:, gs:gs + NSA_HD]) - slope * dist_cf
        e, d = _masked_exp(s, mask_c)
        p = e * _safe_inv(d)
        o_cmp = _dot(p, vblk[:, gs:gs + NSA_HD])
        psum = p[0:tp]
        for j in range(1, NSA_HPG):
            psum = psum + p[j * tp:(j + 1) * tp]
        imp = lax.dot_general(m_ref[...], psum, (((1,), (1,)), ((), ())),
                              precision=lax.Precision.HIGHEST, preferred_element_type=F32)
        val = jnp.where(force, jnp.inf, jnp.where(elig, imp, -jnp.inf))
        sel = jnp.zeros((n_rows, tp), F32)
        for _ in range(min(N_SEL, n_slc)):
            m = jnp.max(val, axis=0, keepdims=True)
            first = jnp.min(jnp.where(val == m, nidx_f, float(n_rows)), axis=0, keepdims=True)
            pick = nidx_f == first
            sel = jnp.where(pick, 1.0, sel)
            val = jnp.where(pick, -jnp.inf, val)
        sel = jnp.where(nidx < n_slc, sel, 0.0)
        sel_keys = _dot_tn(sel[:n_past_blocks], ex_ref[...])
        mask_p = jnp.concatenate([sel_keys] * NSA_HPG, axis=0) > 0.5
        sel_new = jnp.sum(jnp.where(eye, sel[n_past_blocks:n_past_blocks + 1], 0.0), axis=-1, keepdims=True)
        sel_new = jnp.concatenate([sel_new] * NSA_HPG, axis=0) > 0.5
        s_p = _dot(qg, ks_buf[g]) - slope * dist_pf
        s_n = _dot_nt(qg, ksn_ref[0, :, gs:gs + NSA_HD]) - slope * dist_nf
        mask_sn = mask_n & sel_new
        mx = jnp.maximum(jnp.max(jnp.where(mask_p, s_p, NEG_INF), -1, keepdims=True),
                         jnp.max(jnp.where(mask_sn, s_n, NEG_INF), -1, keepdims=True))
        e_p = jnp.where(mask_p, jnp.exp(jnp.where(mask_p, s_p, NEG_INF) - mx), 0.0)
        e_n = jnp.where(mask_sn, jnp.exp(jnp.where(mask_sn, s_n, NEG_INF) - mx), 0.0)
        den = jnp.sum(e_p, -1, keepdims=True) + jnp.sum(e_n, -1, keepdims=True)
        o_slc = (_dot_nt(e_p, vs_buf[g]) + _dot(e_n, vsn_ref[0, :, gs:gs + NSA_HD])) * _safe_inv(den)
        s_b = _dot(qg, wk_ref[0, g]) - slope * dist_bf
        s_n = _dot_nt(qg, kwn_ref[0, :, gs:gs + NSA_HD]) - slope * dist_nf
        mx = jnp.maximum(jnp.max(jnp.where(mask_b, s_b, NEG_INF), -1, keepdims=True),
                         jnp.max(jnp.where(mask_n, s_n, NEG_INF), -1, keepdims=True))
        e_b = jnp.where(mask_b, jnp.exp(jnp.where(mask_b, s_b, NEG_INF) - mx), 0.0)
        e_n = jnp.where(mask_n, jnp.exp(jnp.where(mask_n, s_n, NEG_INF) - mx), 0.0)
        den = jnp.sum(e_b, -1, keepdims=True) + jnp.sum(e_n, -1, keepdims=True)
        o_win = (_dot_nt(e_b, wv_ref[0, g]) + _dot(e_n, vwn_ref[0, :, gs:gs + NSA_HD])) * _safe_inv(den)
        for j, h in enumerate(heads):
            rs = slice(j * tp, (j + 1) * tp)
            o_ref[0, :, h * NSA_HD:(h + 1) * NSA_HD] = (
                gates[:, 3 * h:3 * h + 1] * o_cmp[rs]
                + gates[:, 3 * h + 1:3 * h + 2] * o_slc[rs]
                + gates[:, 3 * h + 2:3 * h + 3] * o_win[rs])


def _nsa_sample(page_table, nq, ng, ksn, vsn, kwn, vwn, win_k, win_v, kwts, vwts,
                pool_kc, pool_vc, pool_ks, pool_vs, *, t_valid):
    db, tp, _ = nq.shape
    n_pages = page_table.shape[1]
    past = n_pages * PAGE_SIZE
    n_sub = past // CMP_STRIDE
    n_cmp = n_sub - CMP_BLOCK // CMP_STRIDE + 1
    n_slc = -(-(past + t_valid) // SEL_BLOCK)
    n_rows = -(-n_slc // SUBLANES) * SUBLANES
    n_buf = win_k.shape[3]
    m = np.zeros((n_rows, n_sub), np.float32)
    m[:n_slc, :n_cmp] = _cmp_to_slc(n_cmp, n_slc).T
    n_past_blocks = past // SEL_BLOCK
    expand = (np.arange(past)[None, :] // SEL_BLOCK == np.arange(n_past_blocks)[:, None]).astype(np.float32)
    row_view = lambda p: p.reshape(p.shape[0], PAGE_SIZE, KV_WIDTH)
    col_view = lambda p: jnp.transpose(p, (0, 2, 3, 1))
    per_b = lambda r, w: pl.BlockSpec((1, r, w), lambda i, pt: (i, 0, 0))
    const = lambda a: pl.BlockSpec(a.shape, lambda i, pt: (0,) * a.ndim)
    wts = kwts + vwts
    m_j = jnp.asarray(m)
    ex_j = jnp.asarray(expand, dtype=BF16)
    any_spec = pl.BlockSpec(memory_space=pl.ANY)
    grid_spec = pltpu.PrefetchScalarGridSpec(
        num_scalar_prefetch=1,
        grid=(db,),
        in_specs=[per_b(tp, NSA_WIDTH), per_b(tp, LANES)] + [per_b(tp, KV_WIDTH)] * 4
                 + [pl.BlockSpec((1, NSA_GROUPS, NSA_HD, n_buf), lambda i, pt: (i, 0, 0, 0))] * 2
                 + [const(w) for w in wts] + [const(m_j), const(ex_j)]
                 + [any_spec] * 4,
        out_specs=per_b(tp, NSA_WIDTH),
        scratch_shapes=[pltpu.VMEM((past, KV_WIDTH), F32)] * 2
                       + [pltpu.VMEM((NSA_GROUPS, NSA_HD, past), F32)] * 2
                       + [pltpu.SemaphoreType.DMA((4,))],
    )
    return pl.pallas_call(
        functools.partial(_nsa_sample_kernel, t_valid=t_valid, n_slc=n_slc),
        grid_spec=grid_spec,
        out_shape=jax.ShapeDtypeStruct((db, tp, NSA_WIDTH), F32),
        compiler_params=_params("arbitrary"),
        name="nsa_sample",
    )(page_table, nq, ng, ksn, vsn, kwn, vwn, win_k, win_v, *wts, m_j, ex_j,
      row_view(pool_kc), row_view(pool_vc), col_view(pool_ks), col_view(pool_vs))


def _outproj_kernel(oh_ref, on_ref, x_ref, w_ref, g_ref, b_ref, y_ref):
    mix = (jnp.dot(oh_ref[...].astype(BF16), w_ref[:HG_WIDTH, :], preferred_element_type=F32)
           + jnp.dot(on_ref[...].astype(BF16), w_ref[HG_WIDTH:, :], preferred_element_type=F32))
    y_ref[...] = _layer_norm(DN_ALPHA * x_ref[...] + mix, g_ref[...], b_ref[...])


def _outproj(o_h, o_n, x, w_bf, g, b):
    n = x.shape[0]
    tm = min(ROW_TILE, n)
    row = lambda w: pl.BlockSpec((tm, w), lambda i: (i, 0))
    const = lambda a: pl.BlockSpec(a.shape, lambda i: (0, 0))
    return pl.pallas_call(
        _outproj_kernel,
        grid=(n // tm,),
        in_specs=[row(HG_WIDTH), row(NSA_WIDTH), row(D_MODEL), const(w_bf), const(g), const(b)],
        out_specs=row(D_MODEL),
        out_shape=jax.ShapeDtypeStruct((n, D_MODEL), F32),
        compiler_params=_params("parallel"),
        name="outproj_ln1",
    )(o_h, o_n, x, w_bf, g, b)


def _memkv_kernel(m_ref, wk_ref, wv_ref, k_ref, v_ref):
    mb = m_ref[...].astype(BF16)
    for w_ref, o_ref in ((wk_ref, k_ref), (wv_ref, v_ref)):
        res = jnp.dot(mb, w_ref[...], preferred_element_type=F32)
        for h in range(MEM_HEADS):
            o_ref[:, h, :] = res[:, h * MEM_HD:(h + 1) * MEM_HD]


def _memkv(mem, wk_bf, wv_bf):
    n = mem.shape[0]
    tm = min(ROW_TILE, n)
    row = pl.BlockSpec((tm, D_MODEL), lambda i: (i, 0))
    heads = pl.BlockSpec((tm, MEM_HEADS, MEM_HD), lambda i: (i, 0, 0))
    const = lambda a: pl.BlockSpec(a.shape, lambda i: (0, 0))
    return pl.pallas_call(
        _memkv_kernel,
        grid=(n // tm,),
        in_specs=[row, const(wk_bf), const(wv_bf)],
        out_specs=[heads, heads],
        out_shape=[jax.ShapeDtypeStruct((n, MEM_HEADS, MEM_HD), F32)] * 2,
        compiler_params=_params("parallel"),
        name="mem_kv",
    )(mem, wk_bf, wv_bf)


def _memattn_kernel(x_ref, mk_ref, mv_ref, wq_ref, wo_ref, g_ref, b_ref, y_ref, o_buf, mk_buf, mv_buf):
    nb, tm, _ = x_ref.shape
    @pl.when(pl.program_id(1) == 0)
    def _():
        for i in range(nb):
            for h in range(MEM_HEADS):
                mk_buf[i, h] = mk_ref[i, :, h, :].astype(BF16)
                mv_buf[i, h] = mv_ref[i, :, h, :].astype(BF16)

    x = x_ref[...].reshape(nb * tm, D_MODEL)
    q = jnp.dot(x.astype(BF16), wq_ref[...], preferred_element_type=F32) * (MEM_HD ** -0.5)
    for i in range(nb):
        rs = slice(i * tm, (i + 1) * tm)
        for h in range(MEM_HEADS):
            cs = slice(h * MEM_HD, (h + 1) * MEM_HD)
            s = _dot_nt(q[rs, cs], mk_buf[i, h])
            e = jnp.exp(s - jnp.max(s, -1, keepdims=True))
            o_buf[rs, cs] = _dot(e, mv_buf[i, h]) / jnp.sum(e, -1, keepdims=True)
    att = jnp.dot(o_buf[...].astype(BF16), wo_ref[...], preferred_element_type=F32)
    y = _layer_norm(DN_ALPHA * x + att, g_ref[...], b_ref[...])
    y_ref[...] = y.reshape(nb, tm, D_MODEL)


def _memattn(x, mem_k, mem_v, wq_bf, wo_bf, g, b, *, nb, tm):
    bsz, t, _ = x.shape
    n_mem = mem_k.shape[1]
    const = lambda a: pl.BlockSpec(a.shape, lambda i, j: (0, 0))
    return pl.pallas_call(
        _memattn_kernel,
        grid=(bsz // nb, t // tm),
        in_specs=[pl.BlockSpec((nb, tm, D_MODEL), lambda i, j: (i, j, 0)),
                  pl.BlockSpec((nb, n_mem, MEM_HEADS, MEM_HD), lambda i, j: (i, 0, 0, 0)),
                  pl.BlockSpec((nb, n_mem, MEM_HEADS, MEM_HD), lambda i, j: (i, 0, 0, 0)),
                  const(wq_bf), const(wo_bf), const(g), const(b)],
        out_specs=pl.BlockSpec((nb, tm, D_MODEL), lambda i, j: (i, j, 0)),
        out_shape=jax.ShapeDtypeStruct((bsz, t, D_MODEL), F32),
        scratch_shapes=[pltpu.VMEM((nb * tm, D_MODEL), F32),
                        pltpu.VMEM((nb, MEM_HEADS, n_mem, MEM_HD), BF16),
                        pltpu.VMEM((nb, MEM_HEADS, n_mem, MEM_HD), BF16)],
        compiler_params=_params("parallel", "arbitrary"),
        name="mem_attn_ln2",
    )(x, mem_k, mem_v, wq_bf, wo_bf, g, b)


def _take_top(s, idx, n_take, tie_safe):
    rank = jnp.full(s.shape, float(n_take), F32)
    tops = []
    for i in range(n_take):
        m = jnp.max(s, axis=0, keepdims=True)
        hit = s == m
        if tie_safe:
            first = jnp.min(jnp.where(hit, idx, 1e9), axis=0, keepdims=True)
            hit = idx == first
        rank = jnp.where(hit, float(i), rank)
        s = jnp.where(hit, -jnp.inf, s)
        tops.append(m)
    return rank, tops, s


def _n_removed(s):
    return jnp.sum(jnp.where(s == -jnp.inf, 1.0, 0.0), axis=0, keepdims=True)


def _peer_route(h, lanes, qh_ref, sk_ref, lr_ref, e1_ref, rank2_ref, e2_ref, tie_safe):
    k = PEER_TOPK
    s1 = _dot_nt(sk_ref[h, 0], qh_ref[h, lanes, 0:PEER_KEY_DIM])
    s2 = _dot_nt(sk_ref[h, 1], qh_ref[h, lanes, PEER_KEY_DIM:2 * PEER_KEY_DIM])
    kidx = lax.broadcasted_iota(jnp.int32, (PEER_NKEYS, 1), 0).astype(F32)
    rank1, top1, rest1 = _take_top(s1, kidx, k, tie_safe)
    rank2, top2, rest2 = _take_top(s2, kidx, k, tie_safe)
    t2 = jnp.concatenate(top2, axis=0)
    r8 = lax.broadcasted_iota(jnp.int32, (SUBLANES, 1), 0).astype(F32)
    r16 = lax.broadcasted_iota(jnp.int32, (2 * SUBLANES, 1), 0).astype(F32)
    blocks = [top1[0] + t2]
    flat = [r16]
    off = [jnp.zeros_like(r16)]
    for i in range(1, SUBLANES):
        blocks.append(top1[i] + t2[:SUBLANES])
        flat.append(float(i * k) + r8)
        off.append(jnp.where(r8 < float(k // (i + 1)), 0.0, -jnp.inf))
    blocks.append(jnp.concatenate(top1[SUBLANES:], axis=0) + top2[0])
    flat.append(float(k) * (r8 + float(SUBLANES)))
    off.append(jnp.zeros_like(r8))
    cand = jnp.concatenate(blocks, axis=0) + jnp.concatenate(off, axis=0)
    _, ctops, rest_c = _take_top(cand, jnp.concatenate(flat, axis=0), k, tie_safe)
    cnt = jnp.where(rest_c == -jnp.inf, 1.0, 0.0) - jnp.where(cand == -jnp.inf, 1.0, 0.0)
    z = jnp.sum(cnt * jnp.exp(cand - ctops[0]), axis=0, keepdims=True)
    lrow = jnp.zeros(s1.shape, F32)
    row0 = 0
    for i in range(SUBLANES):
        n_rows = 2 * SUBLANES if i == 0 else SUBLANES
        l_i = jnp.sum(cnt[row0:row0 + n_rows], axis=0, keepdims=True)
        lrow = jnp.where(rank1 == float(i), l_i, lrow)
        row0 += n_rows
    for r in range(SUBLANES):
        lrow = jnp.where(rank1 == float(SUBLANES + r), cnt[row0 + r:row0 + r + 1], lrow)
    lr_ref[h, :, lanes] = lrow
    e1_ref[h, :, lanes] = jnp.exp(s1 - top1[0]) * (1.0 / z)
    rank2_ref[h, :, lanes] = rank2.astype(BF16)
    e2_ref[h, :, lanes] = jnp.exp(s2 - top2[0]).astype(BF16)
    bad = (jnp.abs(_n_removed(rest1) - float(k)) + jnp.abs(_n_removed(rest2) - float(k))
           + jnp.abs(jnp.sum(cnt, axis=0, keepdims=True) - float(k)))
    return jnp.max(bad) > 0.5


def _peer_kernel(x_ref, wq_ref, sk_ref, u_ref, vt_ref, g_ref, b_ref, y_ref,
                 xt_ref, qh_ref, lr_ref, e1_ref, rank2_ref, e2_ref, acc_ref):
    kb = pl.program_id(1)
    tm = x_ref.shape[0]
    eb = u_ref.shape[0]
    a_per_block = eb // PEER_NKEYS

    @pl.when(kb == 0)
    def _():
        x = x_ref[...]
        xt_ref[...] = x.T.astype(BF16)
        acc_ref[...] = jnp.zeros_like(acc_ref)
        qh =jnp.dot(x.astype(BF16), wq_ref[...], preferred_element_type=F32)
        for h in range(PEER_HEADS):
            qh_ref[h] = qh[:, 2 * h * PEER_KEY_DIM:2 * (h + 1) * PEER_KEY_DIM].astype(BF16)
        refs = (qh_ref, sk_ref, lr_ref, e1_ref, rank2_ref, e2_ref)
        lane_tiles = [slice(lt * LANES, (lt + 1) * LANES) for lt in range(tm // LANES)]
        group = PEER_ROUTE_GROUP if len(lane_tiles) % PEER_ROUTE_GROUP == 0 else 1
        for g0 in range(0, len(lane_tiles), group):
            tiles = lane_tiles[g0:g0 + group]

            def route(h, carry, tiles=tiles):
                ties = [_peer_route(h, lanes, *refs, tie_safe=False) for lanes in tiles]
                tie = ties[0]
                for t in ties[1:]:
                    tie = jnp.logical_or(tie, t)

                @pl.when(tie)
                def _():
                    for lanes in tiles:
                        _peer_route(h, lanes, *refs, tie_safe=True)
                return carry

            lax.fori_loop(0, PEER_HEADS, route, 0)

    ht = jnp.dot(u_ref[...], xt_ref[...], preferred_element_type=F32)
    gh = _gelu(ht.astype(BF16))
    wh = []
    for al in range(a_per_block):
        a = kb * a_per_block + al
        w = jnp.zeros((PEER_NKEYS, tm), BF16)
        for h in range(PEER_HEADS):
            lr = jnp.broadcast_to(lr_ref[h, pl.ds(a, 1), :].astype(BF16), (PEER_NKEYS, tm))
            e1 = jnp.broadcast_to(e1_ref[h, pl.ds(a, 1), :].astype(BF16), (PEER_NKEYS, tm))
            w = w + jnp.where(rank2_ref[h] < lr, e2_ref[h], jnp.zeros_like(w)) * e1
        wh.append(w * gh[al * PEER_NKEYS:(al + 1) * PEER_NKEYS])
    acc_ref[...] += jnp.dot(vt_ref[...], jnp.concatenate(wh, axis=0), preferred_element_type=F32)

    @pl.when(kb == pl.num_programs(1) - 1)
    def _():
        y_ref[...] = _layer_norm(DN_ALPHA * x_ref[...] + acc_ref[...].T, g_ref[...], b_ref[...])


def _peer(x, wq_bf, sk_bf, u_bf, vt_bf, g, b):
    n_tokens = x.shape[0]
    n = -(-n_tokens // LANES) * LANES
    x = jnp.pad(x, ((0, n - n_tokens), (0, 0)))
    tm = ROW_TILE if n % ROW_TILE == 0 else LANES
    n_exp = u_bf.shape[0]
    eb = PEER_EXPERT_BLOCK
    once = pl.Buffered(1)
    const2 = lambda a: pl.BlockSpec(a.shape, lambda i, k: (0, 0), pipeline_mode=once)
    return pl.pallas_call(
        _peer_kernel,
        grid=(n // tm, n_exp // eb),
        in_specs=[pl.BlockSpec((tm, D_MODEL), lambda i, k: (i, 0)),
                  const2(wq_bf),
                  pl.BlockSpec(sk_bf.shape, lambda i, k: (0, 0, 0, 0), pipeline_mode=once),
                  pl.BlockSpec((eb, D_MODEL), lambda i, k: (k, 0)),
                  pl.BlockSpec((D_MODEL, eb), lambda i, k: (0, k)),
                  const2(g), const2(b)],
        out_specs=pl.BlockSpec((tm, D_MODEL), lambda i, k: (i, 0)),
        out_shape=jax.ShapeDtypeStruct((n, D_MODEL), F32),
        scratch_shapes=[pltpu.VMEM((D_MODEL, tm), BF16),
                        pltpu.VMEM((PEER_HEADS, tm, 2 * PEER_KEY_DIM), BF16),
                        pltpu.VMEM((PEER_HEADS, PEER_NKEYS, tm), F32),
                        pltpu.VMEM((PEER_HEADS, PEER_NKEYS, tm), F32),
                        pltpu.VMEM((PEER_HEADS, PEER_NKEYS, tm), BF16),
                        pltpu.VMEM((PEER_HEADS, PEER_NKEYS, tm), BF16),
                        pltpu.VMEM((D_MODEL, tm), F32)],
        compiler_params=_params("parallel", "arbitrary"),
        name="peer_ln3",
    )(x, wq_bf, sk_bf, u_bf, vt_bf, g, b)[:n_tokens]


def kernel(x_prompt, x_sample, cache_k_cmp, cache_v_cmp, cache_k_slc, cache_v_slc, cache_k_win, cache_v_win,
           state_hgrn, cache_mem_k, cache_mem_v, page_table, mem_prompt, w_in, hgrn_lb_logits, hgrn_norm_g,
           cmp_k_w1, cmp_k_b1, cmp_k_w2, cmp_k_b2, cmp_v_w1, cmp_v_b1, cmp_v_w2, cmp_v_b2, w_out, ln1_g, ln1_b,
           w_mem_q, w_mem_k, w_mem_v, w_mem_o, ln2_g, ln2_b, peer_w_q, peer_sub_keys, peer_u, peer_v,
           ln3_g, ln3_b):
    bsz, seq, _ = x_prompt.shape
    db, dt, _ = x_sample.shape
    dtp = -(-dt // SUBLANES) * SUBLANES
    layer = 0

    w_in_pad = jnp.pad(w_in[layer], ((0, 0), (0, sum(IN_GROUP_WIDTHS) - IN_COLS))).astype(BF16)
    kv0 = 4 * HG_WIDTH + NSA_WIDTH
    w_in_main = jnp.concatenate([w_in_pad[:, :kv0 + 2 * KV_WIDTH], w_in_pad[:, kv0 + 6 * KV_WIDTH:]], axis=1)
    w_in_kv_t = w_in_pad[:, kv0:kv0 + 6 * KV_WIDTH].T
    kwts = _cmp_weights(cmp_k_w1[layer], cmp_k_b1[layer], cmp_k_w2[layer], cmp_k_b2[layer])
    vwts = _cmp_weights(cmp_v_w1[layer], cmp_v_b1[layer], cmp_v_w2[layer], cmp_v_b2[layer])
    w_out_bf = w_out[layer].astype(BF16)
    wmq, wmk, wmv, wmo = (w[layer].astype(BF16) for w in (w_mem_q, w_mem_k, w_mem_v, w_mem_o))
    pwq = peer_w_q[layer].astype(BF16)
    psk = peer_sub_keys[layer].astype(BF16)
    pu = peer_u[layer].astype(BF16)
    pvt = peer_v[layer].astype(BF16).T
    vec = lambda a: a[layer].reshape(1, D_MODEL)
    g1, b1, g2, b2, g3, b3 = (vec(a) for a in (ln1_g, ln1_b, ln2_g, ln2_b, ln3_g, ln3_b))
    norm_g = hgrn_norm_g[layer]

    n_p = bsz * seq
    xp = x_prompt.reshape(n_p, D_MODEL)
    zh, nq, kc, vc, ngate, kc_t, vc_t, ks_t, vs_t, kw_t, vw_t = _inproj_prompt(xp, w_in_main, w_in_kv_t, bsz, seq)
    per_b = lambda a: a.reshape(bsz, seq, a.shape[-1])
    chunk = min(HG_CHUNK, seq)
    o_h, s_p = _hgrn(per_b(zh), hgrn_lb_logits, norm_g, jnp.zeros((bsz, HG_HEADS, HG_D, HG_D), F32),
                     chunk=chunk, step_tokens=min(HG_STEP_TOKENS, seq), t_valid=chunk)
    kblk, vblk = _cmp_prompt(per_b(kc), per_b(vc), kwts, vwts)
    o_n = _nsa_prompt(per_b(nq), per_b(ngate), kblk, vblk, ks_t, vs_t, kw_t, vw_t)
    x1 = _outproj(o_h.reshape(n_p, HG_WIDTH), o_n.reshape(n_p, NSA_WIDTH), xp, w_out_bf, g1, b1)
    n_mem = mem_prompt.shape[1]
    mem_k, mem_v = _memkv(mem_prompt.reshape(bsz * n_mem, D_MODEL), wmk, wmv)
    mem_k = mem_k.reshape(bsz, n_mem, MEM_HEADS, MEM_HD)
    mem_v = mem_v.reshape(bsz, n_mem, MEM_HEADS, MEM_HD)
    x2 = _memattn(x1.reshape(bsz, seq, D_MODEL), mem_k, mem_v, wmq, wmo, g2, b2, nb=1, tm=min(ROW_TILE, seq))
    y_p = _peer(x2.reshape(n_p, D_MODEL), pwq, psk, pu, pvt, g3, b3).reshape(bsz, seq, D_MODEL)

    kv5 = lambda a: jnp.transpose(a.reshape(bsz, NSA_GROUPS, NSA_HD, a.shape[-1]), (0, 3, 1, 2))[None]
    n_win = min(WINDOW, seq)
    win5 = lambda a: kv5(a[:, :, seq - n_win:])
    mem5 = lambda a: a[None]

    n_s = db * dtp
    xs = jnp.pad(x_sample, ((0, 0), (0, dtp - dt), (0, 0))).reshape(n_s, D_MODEL)
    zh, nq, kc_s, vc_s, ks_s, vs_s, kw_s, vw_s, ngate = _inproj(xs, w_in_pad)
    per_s = lambda a: a.reshape(db, dtp, a.shape[-1])
    o_h, s_s = _hgrn(per_s(zh), hgrn_lb_logits, norm_g, state_hgrn[layer],
                     chunk=dtp, step_tokens=dtp, t_valid=dt)
    o_n = _nsa_sample(page_table, per_s(nq), per_s(ngate), per_s(ks_s), per_s(vs_s), per_s(kw_s), per_s(vw_s),
                      jnp.transpose(cache_k_win[layer], (0, 2, 3, 1)), jnp.transpose(cache_v_win[layer], (0, 2, 3, 1)),
                      kwts, vwts, cache_k_cmp[layer], cache_v_cmp[layer], cache_k_slc[layer], cache_v_slc[layer],
                      t_valid=dt)
    x1 = _outproj(o_h.reshape(n_s, HG_WIDTH), o_n.reshape(n_s, NSA_WIDTH), xs, w_out_bf, g1, b1)
    nb = min(4, db)
    x2 = _memattn(x1.reshape(db, dtp, D_MODEL), cache_mem_k[layer], cache_mem_v[layer],
                  wmq, wmo, g2, b2, nb=nb, tm=dtp)
    y_s = _peer(x2.reshape(n_s, D_MODEL), pwq, psk, pu, pvt, g3, b3).reshape(db, dtp, D_MODEL)[:, :dt]
    skv5 = lambda a: a.reshape(db, dtp, NSA_GROUPS, NSA_HD)[None, :, :dt]

    return (y_p, y_s, kv5(kc_t), kv5(vc_t), kv5(ks_t), kv5(vs_t), win5(kw_t), win5(vw_t), s_p[None],
            mem5(mem_k), mem5(mem_v),
            skv5(kc_s), skv5(vc_s), skv5(ks_s), skv5(vs_s), skv5(kw_s), skv5(vw_s), s_s[None])
```

```python
import functools

import numpy as np
import jax
import jax.numpy as jnp
from jax import lax
from jax.experimental import pallas as pl
from jax.experimental.pallas import tpu as pltpu

F32 = jnp.float32
BF16 = jnp.bfloat16

D_MODEL = 1024
HG_HEADS = 4
HG_D = 128
HG_WIDTH = HG_HEADS * HG_D
NSA_HEADS = 8
NSA_GROUPS = 2
NSA_HPG = NSA_HEADS // NSA_GROUPS
NSA_HD = 64
NSA_WIDTH = NSA_HEADS * NSA_HD
KV_WIDTH = NSA_GROUPS * NSA_HD
CMP_BLOCK = 32
CMP_STRIDE = 16
CMP_HIDDEN = 128
SEL_BLOCK = 64
N_SEL = 16
WINDOW = 512
PAGE_SIZE = 128
MEM_HEADS = 4
MEM_HD = D_MODEL // MEM_HEADS
PEER_HEADS = 8
PEER_NKEYS = 128
PEER_KEY_DIM = 128
PEER_TOPK = 16
DN_ALPHA = 2.0 ** 0.25
LN_EPS = 1e-5
NEG_INF = -1e30
LOG2E = 1.4426950408889634
ALIBI_SLOPES = tuple(2.0 ** (-8.0 * (h + 1) / NSA_HEADS) for h in range(NSA_HEADS))

LANES = 128
SUBLANES = 8
VMEM_LIMIT_BYTES = 56 * 1024 * 1024

HG_CHUNK = 32
HG_STEP_TOKENS = 256
HG_SEQS_PER_STEP = 2
NSA_Q_TILE = 128
ROW_TILE = 512
PEER_EXPERT_BLOCK = 2048
PEER_ROUTE_GROUP = 4
SLC_KEY_CHUNK = 512

IN_GROUP_WIDTHS = (4 * HG_WIDTH, NSA_WIDTH) + (KV_WIDTH,) * 6 + (LANES,)
IN_COLS = 4 * HG_WIDTH + NSA_WIDTH + 6 * KV_WIDTH + 3 * NSA_HEADS


def _params(*sem):
    return pltpu.CompilerParams(dimension_semantics=sem, vmem_limit_bytes=VMEM_LIMIT_BYTES)


def _gelu(x):
    return 0.5 * x * (1.0 + jnp.tanh(0.7978845608028654 * (x + 0.044715 * (x * x * x))))


def _layer_norm(y, g, b):
    mu = jnp.mean(y, -1, keepdims=True)
    yc = y - mu
    var = jnp.mean(yc * yc, -1, keepdims=True)
    return yc * lax.rsqrt(var + LN_EPS) * g + b


def _dot(a, b):
    return jnp.dot(a.astype(BF16), b.astype(BF16), preferred_element_type=F32)


def _dot_nt(a, b):
    return lax.dot_general(a.astype(BF16), b.astype(BF16), (((1,), (1,)), ((), ())),
                           preferred_element_type=F32)


def _dot_tn(a, b):
    return lax.dot_general(a.astype(BF16), b.astype(BF16), (((0,), (0,)), ((), ())),
                           preferred_element_type=F32)


def _masked_exp(s, mask):
    sm = jnp.where(mask, s, NEG_INF)
    m = jnp.max(sm, -1, keepdims=True)
    e = jnp.where(mask, jnp.exp(sm - m), 0.0)
    return e, jnp.sum(e, -1, keepdims=True)


def _safe_inv(d):
    return jnp.where(d > 0.0, 1.0 / jnp.where(d > 0.0, d, 1.0), 0.0)


def _inproj_kernel(x_ref, w_ref, *out_refs):
    xb = x_ref[...].astype(BF16)
    off = 0
    for o_ref in out_refs:
        width = o_ref.shape[-1]
        o_ref[...] = jnp.dot(xb, w_ref[:, off:off + width], preferred_element_type=F32)
        off += width


def _inproj(x, w_pad):
    n = x.shape[0]
    tm = min(ROW_TILE, n)
    return pl.pallas_call(
        _inproj_kernel,
        grid=(n // tm,),
        in_specs=[pl.BlockSpec((tm, D_MODEL), lambda i: (i, 0)),
                  pl.BlockSpec(w_pad.shape, lambda i: (0, 0))],
        out_specs=[pl.BlockSpec((tm, w), lambda i: (i, 0)) for w in IN_GROUP_WIDTHS],
        out_shape=[jax.ShapeDtypeStruct((n, w), F32) for w in IN_GROUP_WIDTHS],
        compiler_params=_params("parallel"),
        name="inproj",
    )(x, w_pad)


def _inproj_prompt_kernel(x_ref, w_ref, wkv_t_ref, zh_ref, nq_ref, kc_ref, vc_ref, ng_ref, *kv_t_refs):
    xb = x_ref[...].astype(BF16)
    off = 0
    for o_ref in (zh_ref, nq_ref, kc_ref, vc_ref, ng_ref):
        width = o_ref.shape[-1]
        o_ref[...] = jnp.dot(xb, w_ref[:, off:off + width], preferred_element_type=F32)
        off += width
    z_t = lax.dot_general(wkv_t_ref[...], xb, (((1,), (1,)), ((), ())), preferred_element_type=F32)
    for i, o_ref in enumerate(kv_t_refs):
        o_ref[0] = z_t[i * KV_WIDTH:(i + 1) * KV_WIDTH]


def _inproj_prompt(x, w_main, wkv_t, bsz, seq):
    n = x.shape[0]
    tm = min(ROW_TILE, seq)
    tiles = seq // tm
    widths = (4 * HG_WIDTH, NSA_WIDTH, KV_WIDTH, KV_WIDTH, LANES)
    n_kv = wkv_t.shape[0] // KV_WIDTH
    return pl.pallas_call(
        _inproj_prompt_kernel,
        grid=(n // tm,),
        in_specs=[pl.BlockSpec((tm, D_MODEL), lambda i: (i, 0)),
                  pl.BlockSpec(w_main.shape, lambda i: (0, 0)),
                  pl.BlockSpec(wkv_t.shape, lambda i: (0, 0))],
        out_specs=[pl.BlockSpec((tm, w), lambda i: (i, 0)) for w in widths]
                  + [pl.BlockSpec((1, KV_WIDTH, tm), lambda i: (i // tiles, 0, i % tiles))] * n_kv,
        out_shape=[jax.ShapeDtypeStruct((n, w), F32) for w in widths]
                  + [jax.ShapeDtypeStruct((bsz, KV_WIDTH, seq), F32)] * n_kv,
        compiler_params=_params("parallel"),
        name="inproj_prompt",
    )(x, w_main, wkv_t)


def _hgrn_kernel(zh_ref, lbl_ref, ng_ref, s0_ref, tri_ref, o_ref, sout_ref, st_ref, obuf_ref,
                 *, chunk, t_valid):
    ti = pl.program_id(1)
    n_seq, step_tokens, _ = zh_ref.shape
    units = [(s, h) for s in range(n_seq) for h in range(HG_HEADS)]

    @pl.when(ti == 0)
    def _():
        for s, h in units:
            st_ref[s, h] = s0_ref[s, h].T

    logits = lbl_ref[...]
    le = jnp.exp(logits - jnp.max(logits, axis=0, keepdims=True))
    lb_all = le[0:1] / jnp.sum(le, axis=0, keepdims=True)
    norm_g = ng_ref[...]
    row = lax.broadcasted_iota(jnp.int32, (chunk, 1), 0)
    row8 = lax.broadcasted_iota(jnp.int32, (SUBLANES, 1), 0)
    n_tiles = chunk // SUBLANES
    tile_match = (lax.broadcasted_iota(jnp.int32, (chunk, n_tiles * chunk), 0) // SUBLANES
                  == lax.broadcasted_iota(jnp.int32, (chunk, n_tiles * chunk), 1) // chunk)

    def do_chunk(c, carry):
        r0 = pl.multiple_of(c * chunk, chunk)
        rows = pl.ds(r0, chunk)
        for s, h in units:
            cs = h * HG_D
            q = zh_ref[s, rows, cs:cs + HG_D]
            zf = zh_ref[s, rows, HG_WIDTH + cs:HG_WIDTH + cs + HG_D]
            v = zh_ref[s, rows, 2 * HG_WIDTH + cs:2 * HG_WIDTH + cs + HG_D]
            zg = zh_ref[s, rows, 3 * HG_WIDTH + cs:3 * HG_WIDTH + cs + HG_D]
            lb = lb_all[:, cs:cs + HG_D]
            lf = jnp.log(lb + (1.0 - lb) * jax.nn.sigmoid(zf))
            k = (1.0 - lb) * jax.nn.sigmoid(-zf)
            if t_valid < chunk:
                lf = jnp.where(row < t_valid, lf, 0.0)
                k = jnp.where(row < t_valid, k, 0.0)
            if chunk <= SUBLANES:
                g = jnp.where(row >= 0, lf[0:1], 0.0)
                for t in range(1, chunk):
                    g = g + jnp.where(row >= t, lf[t:t + 1], 0.0)
            else:
                g = jnp.dot(tri_ref[...], lf, precision=lax.Precision.HIGHEST, preferred_element_type=F32)
            st = st_ref[s, h]
            o = _dot_nt(q * jnp.exp(g), st)
            for t in range(chunk):
                r0 = t // SUBLANES * SUBLANES
                rel = jnp.where(row8 <= t - r0, g[t:t + 1] - g[r0:r0 + SUBLANES], NEG_INF)
                p = (q[t:t + 1] * k[r0:r0 + SUBLANES]) * jnp.exp(rel)
                w = jnp.sum(p, axis=-1, keepdims=True)
                obuf_ref[s, h, t:t + 1, :] = jnp.sum(w * v[r0:r0 + SUBLANES], axis=0, keepdims=True)
            o = o + obuf_ref[s, h]
            if n_tiles > 1:
                g_ref_rows = jnp.concatenate(
                    [jnp.broadcast_to(g[max(i * SUBLANES - 1, 0):max(i * SUBLANES - 1, 0) + 1], (SUBLANES, HG_D))
                     for i in range(n_tiles)], axis=0)
                qa = q * jnp.exp(g - g_ref_rows)
                ka = [jnp.zeros_like(k)]
                for i in range(1, n_tiles):
                    shift = jnp.where(row < i * SUBLANES, g[i * SUBLANES - 1:i * SUBLANES] - g, NEG_INF)
                    ka.append(k * jnp.exp(shift))
                att = _dot_nt(qa, jnp.concatenate(ka, axis=0))
                att = jnp.where(tile_match, att, 0.0)
                o = o + _dot(att, jnp.concatenate([v] * n_tiles, axis=0))
            g_end = g[chunk - 1:chunk]
            kk = k * jnp.exp(g_end - g)
            st_ref[s, h] = st * jnp.exp(g_end) + _dot_tn(v, kk)
            o = o * lax.rsqrt(jnp.mean(o * o, -1, keepdims=True) + LN_EPS) * norm_g
            o_ref[s, rows, cs:cs + HG_D] = o * jax.nn.sigmoid(zg)
        return carry

    lax.fori_loop(0, step_tokens // chunk, do_chunk, 0)

    @pl.when(ti == pl.num_programs(1) - 1)
    def _():
        for s, h in units:
            sout_ref[s, h] = st_ref[s, h].T


def _hgrn(zh, lb_logits, norm_g, s0, *, chunk, step_tokens, t_valid):
    b, t, _ = zh.shape
    n_seq = HG_SEQS_PER_STEP if b % HG_SEQS_PER_STEP == 0 else 1
    tri = jnp.asarray(np.tril(np.ones((chunk, chunk), np.float32)))
    kern = functools.partial(_hgrn_kernel, chunk=chunk, t_valid=t_valid)
    return pl.pallas_call(
        kern,
        grid=(b // n_seq, t // step_tokens),
        in_specs=[pl.BlockSpec((n_seq, step_tokens, 4 * HG_WIDTH), lambda i, j: (i, j, 0)),
                  pl.BlockSpec(lb_logits.shape, lambda i, j: (0, 0)),
                  pl.BlockSpec((1, HG_D), lambda i, j: (0, 0)),
                  pl.BlockSpec((n_seq, HG_HEADS, HG_D, HG_D), lambda i, j: (i, 0, 0, 0)),
                  pl.BlockSpec((chunk, chunk), lambda i, j: (0, 0))],
        out_specs=[pl.BlockSpec((n_seq, step_tokens, HG_WIDTH), lambda i, j: (i, j, 0)),
                   pl.BlockSpec((n_seq, HG_HEADS, HG_D, HG_D), lambda i, j: (i, 0, 0, 0))],
        out_shape=[jax.ShapeDtypeStruct((b, t, HG_WIDTH), F32),
                   jax.ShapeDtypeStruct((b, HG_HEADS, HG_D, HG_D), F32)],
        scratch_shapes=[pltpu.VMEM((n_seq, HG_HEADS, HG_D, HG_D), F32),
                        pltpu.VMEM((n_seq, HG_HEADS, chunk, HG_D), F32)],
        compiler_params=_params("parallel", "arbitrary"),
        name="hgrn",
    )(zh, lb_logits, norm_g.reshape(1, HG_D), s0, tri)


def _cmp_blocks(view, w1_ref, b1_ref, w2_ref, b2_ref):
    return _cmp_finish(_dot(view, w1_ref[...]), b1_ref, w2_ref, b2_ref)


def _cmp_blocks_from_rows(rows_ref, w1_ref, b1_ref, w2_ref, b2_ref):
    n_sub = rows_ref.shape[0] // CMP_STRIDE
    a = None
    for p in range(CMP_STRIDE):
        part = _dot(rows_ref[pl.ds(p, n_sub, stride=CMP_STRIDE), :], w1_ref[p * KV_WIDTH:(p + 1) * KV_WIDTH, :])
        a = part if a is None else a + part
    return _cmp_finish(a, b1_ref, w2_ref, b2_ref)


def _cmp_finish(a, b1_ref, w2_ref, b2_ref):
    n_sub = a.shape[0]
    hid = []
    for g in range(NSA_GROUPS):
        a0 = a[:, (2 * g) * CMP_HIDDEN:(2 * g + 1) * CMP_HIDDEN]
        a1 = a[:, (2 * g + 1) * CMP_HIDDEN:(2 * g + 2) * CMP_HIDDEN]
        hid.append(a0 + pltpu.roll(a1, n_sub - 1, axis=0))
    h = jnp.concatenate(hid, axis=-1) + b1_ref[...]
    return _dot(_gelu(h), w2_ref[...]) + b2_ref[...]


def _cmp_kernel(kv_ref, vv_ref, kw1, kb1, kw2, kb2, vw1, vb1, vw2, vb2, ko_ref, vo_ref):
    ko_ref[0] = _cmp_blocks(kv_ref[0], kw1, kb1, kw2, kb2)
    vo_ref[0] = _cmp_blocks(vv_ref[0], vw1, vb1, vw2, vb2)


def _cmp_weights(w1, b1, w2, b2):
    w1r = w1.reshape(CMP_BLOCK // CMP_STRIDE, CMP_STRIDE, NSA_HD, CMP_HIDDEN)
    big = jnp.zeros((CMP_STRIDE, NSA_GROUPS, NSA_HD, NSA_GROUPS, 2, CMP_HIDDEN), F32)
    for g in range(NSA_GROUPS):
        big = big.at[:, g, :, g, :, :].set(jnp.transpose(w1r, (1, 2, 0, 3)))
    w1b = big.reshape(CMP_STRIDE * KV_WIDTH, NSA_GROUPS * 2 * CMP_HIDDEN).astype(BF16)
    w2b = jnp.zeros((NSA_GROUPS, CMP_HIDDEN, NSA_GROUPS, NSA_HD), F32)
    for g in range(NSA_GROUPS):
        w2b = w2b.at[g, :, g, :].set(w2)
    w2b = w2b.reshape(NSA_GROUPS * CMP_HIDDEN, KV_WIDTH).astype(BF16)
    b1b = jnp.tile(b1, NSA_GROUPS).reshape(1, NSA_GROUPS * CMP_HIDDEN)
    b2b = jnp.tile(b2, NSA_GROUPS).reshape(1, KV_WIDTH)
    return w1b, b1b, w2b, b2b


def _cmp_prompt(kc, vc, kwts, vwts):
    b, t, _ = kc.shape
    n_sub = t // CMP_STRIDE
    view = lambda a: a.reshape(b, n_sub, CMP_STRIDE * KV_WIDTH)
    wspecs = [pl.BlockSpec(w.shape, lambda i: (0, 0)) for w in kwts + vwts]
    return pl.pallas_call(
        _cmp_kernel,
        grid=(b,),
        in_specs=[pl.BlockSpec((1, n_sub, CMP_STRIDE * KV_WIDTH), lambda i: (i, 0, 0))] * 2 + wspecs,
        out_specs=[pl.BlockSpec((1, n_sub, KV_WIDTH), lambda i: (i, 0, 0))] * 2,
        out_shape=[jax.ShapeDtypeStruct((b, n_sub, KV_WIDTH), F32)] * 2,
        compiler_params=_params("parallel"),
        name="cmp_prompt",
    )(view(kc), view(vc), *kwts, *vwts)


def _cmp_to_slc(n_cmp, n_slc):
    cs = np.arange(n_cmp)[:, None] * CMP_STRIDE
    ss = np.arange(n_slc)[None, :] * SEL_BLOCK
    shared = np.clip(np.minimum(cs + CMP_BLOCK, ss + SEL_BLOCK) - np.maximum(cs, ss), 0, None)
    return (shared / CMP_STRIDE).astype(np.float32)


def _softmax2(s):
    m = jnp.max(s, -1, keepdims=True)
    e = jnp.exp2(s - m)
    inv = jnp.where(m > 0.5 * NEG_INF, 1.0 / jnp.sum(e, -1, keepdims=True), 0.0)
    return e, inv


def _nsa_prompt_kernel(q_ref, gate_ref, kb_ref, vb_ref, ks_ref, vs_ref, kw_ref, vw_ref, mt_ref, ex_ref,
                       o_ref, bias_ref, *, n_cmp, n_var):
    tq = q_ref.shape[1]
    t_all = ks_ref.shape[2]
    n_sub = kb_ref.shape[1]
    n_slc = mt_ref.shape[0]
    tile = pl.program_id(1)
    t0 = tile * tq
    q = q_ref[0] * (NSA_HD ** -0.5 * LOG2E)
    slopes = tuple(s * LOG2E for s in ALIBI_SLOPES)
    gates = jax.nn.sigmoid(gate_ref[0])
    qpos = t0 + lax.broadcasted_iota(jnp.int32, (tq, 1), 0)

    def stack_heads(g):
        return jnp.concatenate([q[:, h * NSA_HD:(h + 1) * NSA_HD] for h in range(g * NSA_HPG, (g + 1) * NSA_HPG)],
                               axis=0)

    cidx = lax.broadcasted_iota(jnp.int32, (1, n_sub), 1)
    kend =cidx * CMP_STRIDE + CMP_BLOCK - 1
    bias_c = jnp.where((qpos - kend >= 0) & (cidx < n_cmp), 0.0, NEG_INF)
    kend_rel = (kend - t0).astype(F32)
    nidx = lax.broadcasted_iota(jnp.int32, (n_slc, 1), 0)
    tpos = t0 + lax.broadcasted_iota(jnp.int32, (1, tq), 1)
    elig = nidx * SEL_BLOCK <= tpos
    force = (nidx == tpos // SEL_BLOCK) | (nidx == 0)
    causal = qpos - lax.broadcasted_iota(jnp.int32, (1, t_all), 1) >= 0
    span = min(WINDOW + tq, t_all)
    start = pl.multiple_of(jnp.clip(t0 - WINDOW, 0, t_all - span), tq)
    kpos_w = start + lax.broadcasted_iota(jnp.int32, (1, span), 1)
    bias_w = jnp.where((qpos - kpos_w >= 0) & (qpos - kpos_w < WINDOW), 0.0, NEG_INF)
    kpos_w_rel = (kpos_w - t0).astype(F32)

    for g in range(NSA_GROUPS):
        gs = g * NSA_HD
        kb = kb_ref[0, :, gs:gs + NSA_HD]
        vb = vb_ref[0, :, gs:gs + NSA_HD]
        kw = kw_ref[0, gs:gs + NSA_HD, pl.ds(start, span)]
        vw = vw_ref[0, gs:gs + NSA_HD, pl.ds(start, span)]
        heads = range(g * NSA_HPG, (g + 1) * NSA_HPG)
        qg = stack_heads(g)
        e, inv = _softmax2(_dot_nt(qg, kb) + jnp.concatenate([bias_c + slopes[h] * kend_rel for h in heads], axis=0))
        p = e * inv
        o_cmp = _dot(p, vb)
        psum = p[0:tq]
        for j in range(1, NSA_HPG):
            psum = psum + p[j * tq:(j + 1) * tq]
        e, inv = _softmax2(_dot(qg, kw) + jnp.concatenate([bias_w + slopes[h] * kpos_w_rel for h in heads], axis=0))
        o_win = _dot_nt(e, vw) * inv
        for j, h in enumerate(heads):
            rs = slice(j * tq, (j + 1) * tq)
            o_ref[0, :, h * NSA_HD:(h + 1) * NSA_HD] = (
                gates[:, 3 * h:3 * h + 1] * o_cmp[rs] + gates[:, 3 * h + 2:3 * h + 3] * o_win[rs])
        imp_t = lax.dot_general(mt_ref[...], psum, (((1,), (1,)), ((), ())),
                                precision=lax.Precision.HIGHEST, preferred_element_type=F32)
        val = jnp.where(force, jnp.inf, jnp.where(elig, imp_t, -jnp.inf))
        cnt = jnp.zeros((n_slc, tq), F32)
        for m in range(n_slc):
            other = val[m:m + 1]
            tie = jnp.where(nidx > m, 1.0, 0.0)
            cnt = cnt + jnp.where(other > val, 1.0, jnp.where(other == val, tie, 0.0))
        sel_t = jnp.where(cnt < float(min(N_SEL, n_slc)), 1.0, 0.0)
        sel_keys = _dot(sel_t.T, ex_ref[...])
        bias_ref[g] = jnp.where((sel_keys > 0.5) & causal, 0.0, NEG_INF)

    tiles_per_var = (t_all // tq) // n_var
    for v in range(n_var):
        n_keys = (v + 1) * tiles_per_var * tq

        @pl.when(tile // tiles_per_var == v)
        def _(n_keys=n_keys):
            kpos_rel = (lax.broadcasted_iota(jnp.int32, (1, n_keys), 1) - t0).astype(F32)
            for g in range(NSA_GROUPS):
                gs = g * NSA_HD
                ks = ks_ref[0, gs:gs + NSA_HD, 0:n_keys]
                vs = vs_ref[0, gs:gs + NSA_HD, 0:n_keys]
                bias_s = bias_ref[g, :, 0:n_keys]
                heads = range(g * NSA_HPG, (g + 1) * NSA_HPG)
                bias = jnp.concatenate([bias_s + slopes[h] * kpos_rel for h in heads], axis=0)
                e, inv = _softmax2(_dot(stack_heads(g), ks) + bias)
                o_slc = _dot_nt(e, vs) * inv
                for j, h in enumerate(heads):
                    o_ref[0, :, h * NSA_HD:(h + 1) * NSA_HD] += gates[:, 3 * h + 1:3 * h + 2] * o_slc[j * tq:(j + 1) * tq]


def _nsa_prompt(nq, ng, kblk, vblk, ks, vs, kw, vw):
    b, t, _ = nq.shape
    tq = min(NSA_Q_TILE, t)
    n_sub = kblk.shape[1]
    n_cmp = n_sub - CMP_BLOCK // CMP_STRIDE + 1
    n_slc = -(-t // SEL_BLOCK)
    n_var = max(1, t // SLC_KEY_CHUNK)
    assert (t // tq) % n_var == 0
    mt = np.zeros((n_slc, n_sub), np.float32)
    mt[:, :n_cmp] = _cmp_to_slc(n_cmp, n_slc).T
    expand = (np.arange(t)[None, :] // SEL_BLOCK == np.arange(n_slc)[:, None]).astype(np.float32)
    full = lambda w: pl.BlockSpec((1, w, t), lambda i, j: (i, 0, 0))
    return pl.pallas_call(
        functools.partial(_nsa_prompt_kernel, n_cmp=n_cmp, n_var=n_var),
        grid=(b, t // tq),
        in_specs=[pl.BlockSpec((1, tq, NSA_WIDTH), lambda i, j: (i, j, 0)),
                  pl.BlockSpec((1, tq, LANES), lambda i, j: (i, j, 0)),
                  pl.BlockSpec((1, n_sub, KV_WIDTH), lambda i, j: (i, 0, 0)),
                  pl.BlockSpec((1, n_sub, KV_WIDTH), lambda i, j: (i, 0, 0)),
                  full(KV_WIDTH), full(KV_WIDTH), full(KV_WIDTH), full(KV_WIDTH),
                  pl.BlockSpec(mt.shape, lambda i, j: (0, 0)),
                  pl.BlockSpec(expand.shape, lambda i, j: (0, 0))],
        out_specs=pl.BlockSpec((1, tq, NSA_WIDTH), lambda i, j: (i, j, 0)),
        out_shape=jax.ShapeDtypeStruct((b, t, NSA_WIDTH), F32),
        scratch_shapes=[pltpu.VMEM((NSA_GROUPS, tq, t), F32)],
        compiler_params=_params("parallel", "arbitrary"),
        name="nsa_prompt",
    )(nq, ng, kblk, vblk, ks, vs, kw, vw, jnp.asarray(mt), jnp.asarray(expand, dtype=BF16))


def _nsa_sample_kernel(pt_ref, q_ref, gate_ref, ksn_ref, vsn_ref, kwn_ref, vwn_ref, wk_ref, wv_ref,
                       kw1, kb1, kw2, kb2, vw1, vb1, vw2, vb2, m_ref, ex_ref,
                       kc_pool, vc_pool, ks_pool, vs_pool,
                       o_ref, kc_buf, vc_buf, ks_buf, vs_buf, sem, *, t_valid, n_slc):
    b = pl.program_id(0)
    n_pages = pt_ref.shape[1]
    tp = q_ref.shape[1]
    past = n_pages * PAGE_SIZE
    n_sub = past // CMP_STRIDE
    n_cmp = n_sub - CMP_BLOCK // CMP_STRIDE + 1
    n_past_blocks = past // SEL_BLOCK

    pools = ((kc_pool, kc_buf, False), (vc_pool, vc_buf, False), (ks_pool, ks_buf, True), (vs_pool, vs_buf, True))
    slot = b % 2

    def page_copy(seq, s, i, j):
        pool, buf, rows_on_lanes = pools[i]
        rows = pl.ds(pl.multiple_of(j * PAGE_SIZE, PAGE_SIZE), PAGE_SIZE)
        dst = buf.at[s, :, :, rows] if rows_on_lanes else buf.at[s, rows]
        return pltpu.make_async_copy(pool.at[pt_ref[seq, j]], dst, sem.at[s, i])

    def start_pages(seq, s):
        def body(j, carry):
            for i in range(len(pools)):
                page_copy(seq, s, i, j).start()
            return carry
        lax.fori_loop(0, n_pages, body, 0)

    def wait_pages(which):
        def body(j, carry):
            for i in which:
                page_copy(b, slot, i, j).wait()
            return carry
        lax.fori_loop(0, n_pages, body, 0)

    @pl.when(b == 0)
    def _():
        start_pages(b, slot)

    @pl.when(b + 1 < pl.num_programs(0))
    def _():
        start_pages(b + 1, 1 - slot)

    wait_pages((0, 1))
    kblk = _cmp_blocks_from_rows(kc_buf.at[slot], kw1, kb1, kw2, kb2)
    vblk = _cmp_blocks_from_rows(vc_buf.at[slot], vw1, vb1, vw2, vb2)
    wait_pages((2, 3))

    q = q_ref[0] * (NSA_HD ** -0.5)
    gates = jax.nn.sigmoid(gate_ref[0])
    trow = lax.broadcasted_iota(jnp.int32, (tp, 1), 0)
    trow_q = jnp.concatenate([trow] * NSA_HPG, axis=0)
    qpos_q = past + trow_q
    cidx = lax.broadcasted_iota(jnp.int32, (1, n_sub), 1)
    dist_c = qpos_q - (cidx * CMP_STRIDE + CMP_BLOCK - 1)
    mask_c = (dist_c >= 0) & (cidx < n_cmp)
    dist_cf = dist_c.astype(F32)
    n_rows = m_ref.shape[0]
    nidx = lax.broadcasted_iota(jnp.int32, (n_rows, 1), 0)
    nidx_f = nidx.astype(F32)
    qpos_t = past + lax.broadcasted_iota(jnp.int32, (1, tp), 1)
    elig = (nidx * SEL_BLOCK <= qpos_t) & (nidx < n_slc)
    force = ((nidx == qpos_t // SEL_BLOCK) | (nidx == 0)) & (nidx < n_slc)
    eye = lax.broadcasted_iota(jnp.int32, (tp, tp), 0) == lax.broadcasted_iota(jnp.int32, (tp, tp), 1)
    kpos = lax.broadcasted_iota(jnp.int32, (1, past), 1)
    dist_pf = (qpos_q - kpos).astype(F32)
    rnew = lax.broadcasted_iota(jnp.int32, (1, tp), 1)
    dist_n = trow_q - rnew
    mask_n = (dist_n >= 0) & (rnew < t_valid)
    dist_nf = dist_n.astype(F32)
    n_buf = wk_ref.shape[3]
    dist_b = qpos_q - (past - n_buf + lax.broadcasted_iota(jnp.int32, (1, n_buf), 1))
    mask_b = (dist_b >= 0) & (dist_b < WINDOW)
    dist_bf = dist_b.astype(F32)

    for g in range(NSA_GROUPS):
        gs = g * NSA_HD
        heads = range(g * NSA_HPG, (g + 1) * NSA_HPG)
        qg = jnp.concatenate([q[:, h * NSA_HD:(h + 1) * NSA_HD] for h in heads], axis=0)
        slope = jnp.concatenate([jnp.full((tp, 1), ALIBI_SLOPES[h], F32) for h in heads], axis=0)
        s = _dot_nt(qg, kblk[:, gs:gs + NSA_HD]) - slope * dist_cf
        e, d = _masked_exp(s, mask_c)
        p = e * _safe_inv(d)
        o_cmp = _dot(p, vblk[:, gs:gs + NSA_HD])
        psum = p[0:tp]
        for j in range(1, NSA_HPG):
            psum = psum + p[j * tp:(j + 1) * tp]
        imp = lax.dot_general(m_ref[...], psum, (((1,), (1,)), ((), ())),
                              precision=lax.Precision.HIGHEST, preferred_element_type=F32)
        val = jnp.where(force, jnp.inf, jnp.where(elig, imp, -jnp.inf))
        sel = jnp.zeros((n_rows, tp), F32)
        for _ in range(min(N_SEL, n_slc)):
            m = jnp.max(val, axis=0, keepdims=True)
            first = jnp.min(jnp.where(val == m, nidx_f, float(n_rows)), axis=0, keepdims=True)
            pick = nidx_f == first
            sel = jnp.where(pick, 1.0, sel)
            val = jnp.where(pick, -jnp.inf, val)
        sel = jnp.where(nidx < n_slc, sel, 0.0)
        sel_keys = _dot_tn(sel[:n_past_blocks], ex_ref[...])
        mask_p = jnp.concatenate([sel_keys] * NSA_HPG, axis=0) > 0.5
        sel_new = jnp.sum(jnp.where(eye, sel[n_past_blocks:n_past_blocks + 1], 0.0), axis=-1, keepdims=True)
        sel_new = jnp.concatenate([sel_new] * NSA_HPG, axis=0) > 0.5
        s_p = _dot(qg, ks_buf[slot, g]) - slope * dist_pf
        s_n = _dot_nt(qg, ksn_ref[0, :, gs:gs + NSA_HD]) - slope * dist_nf
        mask_sn = mask_n & sel_new
        mx = jnp.maximum(jnp.max(jnp.where(mask_p, s_p, NEG_INF), -1, keepdims=True),
                         jnp.max(jnp.where(mask_sn, s_n, NEG_INF), -1, keepdims=True))
        e_p = jnp.where(mask_p, jnp.exp(jnp.where(mask_p, s_p, NEG_INF) - mx), 0.0)
        e_n = jnp.where(mask_sn, jnp.exp(jnp.where(mask_sn, s_n, NEG_INF) - mx), 0.0)
        den = jnp.sum(e_p, -1, keepdims=True) + jnp.sum(e_n, -1, keepdims=True)
        o_slc = (_dot_nt(e_p, vs_buf[slot, g]) + _dot(e_n, vsn_ref[0, :, gs:gs + NSA_HD])) * _safe_inv(den)
        s_b = _dot(qg, wk_ref[0, g]) - slope * dist_bf
        s_n = _dot_nt(qg, kwn_ref[0, :, gs:gs + NSA_HD]) - slope * dist_nf
        mx = jnp.maximum(jnp.max(jnp.where(mask_b, s_b, NEG_INF), -1, keepdims=True),
                         jnp.max(jnp.where(mask_n, s_n, NEG_INF), -1, keepdims=True))
        e_b = jnp.where(mask_b, jnp.exp(jnp.where(mask_b, s_b, NEG_INF) - mx), 0.0)
        e_n = jnp.where(mask_n, jnp.exp(jnp.where(mask_n, s_n, NEG_INF) - mx), 0.0)
        den = jnp.sum(e_b, -1, keepdims=True) + jnp.sum(e_n, -1, keepdims=True)
        o_win = (_dot_nt(e_b, wv_ref[0, g]) + _dot(e_n, vwn_ref[0, :, gs:gs + NSA_HD])) * _safe_inv(den)
        for j, h in enumerate(heads):
            rs = slice(j * tp, (j + 1) * tp)
            o_ref[0, :, h * NSA_HD:(h + 1) * NSA_HD] = (
                gates[:, 3 * h:3 * h + 1] * o_cmp[rs]
                + gates[:, 3 * h + 1:3 * h + 2] * o_slc[rs]
                + gates[:, 3 * h + 2:3 * h + 3] * o_win[rs])


def _nsa_sample(page_table, nq, ng, ksn, vsn, kwn, vwn, win_k, win_v, kwts, vwts,
                pool_kc, pool_vc, pool_ks, pool_vs, *, t_valid):
    db, tp, _ = nq.shape
    n_pages = page_table.shape[1]
    past = n_pages * PAGE_SIZE
    n_sub = past // CMP_STRIDE
    n_cmp = n_sub - CMP_BLOCK // CMP_STRIDE + 1
    n_slc = -(-(past + t_valid) // SEL_BLOCK)
    n_rows = -(-n_slc // SUBLANES) * SUBLANES
    n_buf = win_k.shape[3]
    m = np.zeros((n_rows, n_sub), np.float32)
    m[:n_slc, :n_cmp] = _cmp_to_slc(n_cmp, n_slc).T
    n_past_blocks = past // SEL_BLOCK
    expand = (np.arange(past)[None, :] // SEL_BLOCK == np.arange(n_past_blocks)[:, None]).astype(np.float32)
    row_view = lambda p: p.reshape(p.shape[0], PAGE_SIZE, KV_WIDTH)
    col_view = lambda p: jnp.transpose(p, (0, 2, 3, 1))
    per_b = lambda r, w: pl.BlockSpec((1, r, w), lambda i, pt: (i, 0, 0))
    const = lambda a: pl.BlockSpec(a.shape, lambda i, pt: (0,) * a.ndim, pipeline_mode=pl.Buffered(1))
    wts = kwts + vwts
    m_j = jnp.asarray(m)
    ex_j = jnp.asarray(expand, dtype=BF16)
    any_spec = pl.BlockSpec(memory_space=pl.ANY)
    grid_spec = pltpu.PrefetchScalarGridSpec(
        num_scalar_prefetch=1,
        grid=(db,),
        in_specs=[per_b(tp, NSA_WIDTH), per_b(tp, LANES)] + [per_b(tp, KV_WIDTH)] * 4
                 + [pl.BlockSpec((1, NSA_GROUPS, NSA_HD, n_buf), lambda i, pt: (i, 0, 0, 0))] * 2
                 + [const(w) for w in wts] + [const(m_j), const(ex_j)]
                 + [any_spec] * 4,
        out_specs=per_b(tp, NSA_WIDTH),
        scratch_shapes=[pltpu.VMEM((2, past, KV_WIDTH), F32)] * 2
                       + [pltpu.VMEM((2, NSA_GROUPS, NSA_HD, past), F32)] * 2
                       + [pltpu.SemaphoreType.DMA((2, 4))],
    )
    return pl.pallas_call(
        functools.partial(_nsa_sample_kernel, t_valid=t_valid, n_slc=n_slc),
        grid_spec=grid_spec,
        out_shape=jax.ShapeDtypeStruct((db, tp, NSA_WIDTH), F32),
        compiler_params=_params("arbitrary"),
        name="nsa_sample",
    )(page_table, nq, ng, ksn, vsn, kwn, vwn, win_k, win_v, *wts, m_j, ex_j,
      row_view(pool_kc), row_view(pool_vc), col_view(pool_ks), col_view(pool_vs))


def _outproj_kernel(oh_ref, on_ref, x_ref, w_ref, g_ref, b_ref, y_ref):
    mix = (jnp.dot(oh_ref[...].astype(BF16), w_ref[:HG_WIDTH, :], preferred_element_type=F32)
           + jnp.dot(on_ref[...].astype(BF16), w_ref[HG_WIDTH:, :], preferred_element_type=F32))
    y_ref[...] = _layer_norm(DN_ALPHA * x_ref[...] + mix, g_ref[...], b_ref[...])


def _outproj(o_h, o_n, x, w_bf, g, b):
    n = x.shape[0]
    tm = min(ROW_TILE, n)
    row = lambda w: pl.BlockSpec((tm, w), lambda i: (i, 0))
    const = lambda a: pl.BlockSpec(a.shape, lambda i: (0, 0))
    return pl.pallas_call(
        _outproj_kernel,
        grid=(n // tm,),
        in_specs=[row(HG_WIDTH), row(NSA_WIDTH), row(D_MODEL), const(w_bf), const(g), const(b)],
        out_specs=row(D_MODEL),
        out_shape=jax.ShapeDtypeStruct((n, D_MODEL), F32),
        compiler_params=_params("parallel"),
        name="outproj_ln1",
    )(o_h, o_n, x, w_bf, g, b)


def _memkv_kernel(m_ref, wk_ref, wv_ref, k_ref, v_ref):
    mb = m_ref[...].astype(BF16)
    for w_ref, o_ref in ((wk_ref, k_ref), (wv_ref, v_ref)):
        res = jnp.dot(mb, w_ref[...], preferred_element_type=F32)
        for h in range(MEM_HEADS):
            o_ref[:, h, :] = res[:, h * MEM_HD:(h + 1) * MEM_HD]


def _memkv(mem, wk_bf, wv_bf):
    n = mem.shape[0]
    tm = min(ROW_TILE, n)
    row = pl.BlockSpec((tm, D_MODEL), lambda i: (i, 0))
    heads = pl.BlockSpec((tm, MEM_HEADS, MEM_HD), lambda i: (i, 0, 0))
    const = lambda a: pl.BlockSpec(a.shape, lambda i: (0, 0))
    return pl.pallas_call(
        _memkv_kernel,
        grid=(n // tm,),
        in_specs=[row, const(wk_bf), const(wv_bf)],
        out_specs=[heads, heads],
        out_shape=[jax.ShapeDtypeStruct((n, MEM_HEADS, MEM_HD), F32)] * 2,
        compiler_params=_params("parallel"),
        name="mem_kv",
    )(mem, wk_bf, wv_bf)


def _memattn_kernel(x_ref, mk_ref, mv_ref, wq_ref, wo_ref, g_ref, b_ref, y_ref, o_buf, mk_buf, mv_buf):
    nb, tm, _ = x_ref.shape
    @pl.when(pl.program_id(1) == 0)
    def _():
        for i in range(nb):
            for h in range(MEM_HEADS):
                mk_buf[i, h] = mk_ref[i, :, h, :].astype(BF16)
                mv_buf[i, h] = mv_ref[i, :, h, :].astype(BF16)

    x = x_ref[...].reshape(nb * tm, D_MODEL)
    q = jnp.dot(x.astype(BF16), wq_ref[...], preferred_element_type=F32) * (MEM_HD ** -0.5)
    for i in range(nb):
        rs = slice(i * tm, (i + 1) * tm)
        for h in range(MEM_HEADS):
            cs = slice(h * MEM_HD, (h + 1) * MEM_HD)
            s = _dot_nt(q[rs, cs], mk_buf[i, h])
            e = jnp.exp(s - jnp.max(s, -1, keepdims=True))
            o_buf[rs, cs] = _dot(e, mv_buf[i, h]) / jnp.sum(e, -1, keepdims=True)
    att = jnp.dot(o_buf[...].astype(BF16), wo_ref[...], preferred_element_type=F32)
    y = _layer_norm(DN_ALPHA * x + att, g_ref[...], b_ref[...])
    y_ref[...] = y.reshape(nb, tm, D_MODEL)


def _memattn(x, mem_k, mem_v, wq_bf, wo_bf, g, b, *, nb, tm):
    bsz, t, _ = x.shape
    n_mem = mem_k.shape[1]
    const = lambda a: pl.BlockSpec(a.shape, lambda i, j: (0, 0))
    return pl.pallas_call(
        _memattn_kernel,
        grid=(bsz // nb, t // tm),
        in_specs=[pl.BlockSpec((nb, tm, D_MODEL), lambda i, j: (i, j, 0)),
                  pl.BlockSpec((nb, n_mem, MEM_HEADS, MEM_HD), lambda i, j: (i, 0, 0, 0)),
                  pl.BlockSpec((nb, n_mem, MEM_HEADS, MEM_HD), lambda i, j: (i, 0, 0, 0)),
                  const(wq_bf), const(wo_bf), const(g), const(b)],
        out_specs=pl.BlockSpec((nb, tm, D_MODEL), lambda i, j: (i, j, 0)),
        out_shape=jax.ShapeDtypeStruct((bsz, t, D_MODEL), F32),
        scratch_shapes=[pltpu.VMEM((nb * tm, D_MODEL), F32),
                        pltpu.VMEM((nb, MEM_HEADS, n_mem, MEM_HD), BF16),
                        pltpu.VMEM((nb, MEM_HEADS, n_mem, MEM_HD), BF16)],
        compiler_params=_params("parallel", "arbitrary"),
        name="mem_attn_ln2",
    )(x, mem_k, mem_v, wq_bf, wo_bf, g, b)


def _take_top(s, idx, n_take, tie_safe):
    rank = jnp.full(s.shape, float(n_take), F32)
    tops = []
    for i in range(n_take):
        m = jnp.max(s, axis=0, keepdims=True)
        hit = s == m
        if tie_safe:
            first = jnp.min(jnp.where(hit, idx, 1e9), axis=0, keepdims=True)
            hit = idx == first
        rank = jnp.where(hit, float(i), rank)
        s = jnp.where(hit, -jnp.inf, s)
        tops.append(m)
    return rank, tops, s


def _n_removed(s):
    return jnp.sum(jnp.where(s == -jnp.inf, 1.0, 0.0), axis=0, keepdims=True)


def _peer_route(h, lanes, qh_ref, sk_ref, lr_ref, e1_ref, rank2_ref, e2_ref, tie_safe):
    k = PEER_TOPK
    s1 = _dot_nt(sk_ref[h, 0], qh_ref[h, lanes, 0:PEER_KEY_DIM])
    s2 = _dot_nt(sk_ref[h, 1], qh_ref[h, lanes, PEER_KEY_DIM:2 * PEER_KEY_DIM])
    kidx = lax.broadcasted_iota(jnp.int32, (PEER_NKEYS, 1), 0).astype(F32)
    rank1, top1, rest1 = _take_top(s1, kidx, k, tie_safe)
    rank2, top2, rest2 = _take_top(s2, kidx, k, tie_safe)
    t2 = jnp.concatenate(top2, axis=0)
    r8 = lax.broadcasted_iota(jnp.int32, (SUBLANES, 1), 0).astype(F32)
    r16 = lax.broadcasted_iota(jnp.int32, (2 * SUBLANES, 1), 0).astype(F32)
    blocks = [top1[0] + t2]
    flat = [r16]
    off = [jnp.zeros_like(r16)]
    for i in range(1, SUBLANES):
        blocks.append(top1[i] + t2[:SUBLANES])
        flat.append(float(i * k) + r8)
        off.append(jnp.where(r8 < float(k // (i + 1)), 0.0, -jnp.inf))
    blocks.append(jnp.concatenate(top1[SUBLANES:], axis=0) + top2[0])
    flat.append(float(k) * (r8 + float(SUBLANES)))
    off.append(jnp.zeros_like(r8))
    cand = jnp.concatenate(blocks, axis=0) + jnp.concatenate(off, axis=0)
    _, ctops, rest_c = _take_top(cand, jnp.concatenate(flat, axis=0), k, tie_safe)
    cnt = jnp.where(rest_c == -jnp.inf, 1.0, 0.0) - jnp.where(cand == -jnp.inf, 1.0, 0.0)
    z = jnp.sum(cnt * jnp.exp(cand - ctops[0]), axis=0, keepdims=True)
    lrow = jnp.zeros(s1.shape, F32)
    row0 = 0
    for i in range(SUBLANES):
        n_rows = 2 * SUBLANES if i == 0 else SUBLANES
        l_i = jnp.sum(cnt[row0:row0 + n_rows], axis=0, keepdims=True)
        lrow = jnp.where(rank1 == float(i), l_i, lrow)
        row0 += n_rows
    for r in range(SUBLANES):
        lrow = jnp.where(rank1 == float(SUBLANES + r), cnt[row0 + r:row0 + r + 1], lrow)
    lr_ref[h, :, lanes] = lrow
    e1_ref[h, :, lanes] = jnp.exp(s1 - top1[0]) * (1.0 / z)
    rank2_ref[h, :, lanes] = rank2.astype(BF16)
    e2_ref[h, :, lanes] = jnp.exp(s2 - top2[0]).astype(BF16)
    bad = (jnp.abs(_n_removed(rest1) - float(k)) + jnp.abs(_n_removed(rest2) - float(k))
           + jnp.abs(jnp.sum(cnt, axis=0, keepdims=True) - float(k)))
    return jnp.max(bad) > 0.5


def _peer_kernel(x_ref, wq_ref, sk_ref, u_ref, vt_ref, g_ref, b_ref, y_ref,
                 xt_ref, qh_ref, lr_ref, e1_ref, rank2_ref, e2_ref, acc_ref):
    kb = pl.program_id(1)
    tm = x_ref.shape[0]
    eb = u_ref.shape[0]
    a_per_block = eb // PEER_NKEYS

    @pl.when(kb == 0)
    def _():
        x = x_ref[...]
        xt_ref[...] = x.T.astype(BF16)
        acc_ref[...] = jnp.zeros_like(acc_ref)
        qh =jnp.dot(x.astype(BF16), wq_ref[...], preferred_element_type=F32)
        for h in range(PEER_HEADS):
            qh_ref[h] = qh[:, 2 * h * PEER_KEY_DIM:2 * (h + 1) * PEER_KEY_DIM].astype(BF16)
        refs = (qh_ref, sk_ref, lr_ref, e1_ref, rank2_ref, e2_ref)
        lane_tiles = [slice(lt * LANES, (lt + 1) * LANES) for lt in range(tm // LANES)]
        group = PEER_ROUTE_GROUP if len(lane_tiles) % PEER_ROUTE_GROUP == 0 else 1
        for g0 in range(0, len(lane_tiles), group):
            tiles = lane_tiles[g0:g0 + group]

            def route(h, carry, tiles=tiles):
                ties = [_peer_route(h, lanes, *refs, tie_safe=False) for lanes in tiles]
                tie = ties[0]
                for t in ties[1:]:
                    tie = jnp.logical_or(tie, t)

                @pl.when(tie)
                def _():
                    for lanes in tiles:
                        _peer_route(h, lanes, *refs, tie_safe=True)
                return carry

            lax.fori_loop(0, PEER_HEADS, route, 0)

    ht = jnp.dot(u_ref[...], xt_ref[...], preferred_element_type=F32)
    gh = _gelu(ht.astype(BF16))
    wh = []
    for al in range(a_per_block):
        a = kb * a_per_block + al
        w = jnp.zeros((PEER_NKEYS, tm), BF16)
        for h in range(PEER_HEADS):
            lr = jnp.broadcast_to(lr_ref[h, pl.ds(a, 1), :].astype(BF16), (PEER_NKEYS, tm))
            e1 = jnp.broadcast_to(e1_ref[h, pl.ds(a, 1), :].astype(BF16), (PEER_NKEYS, tm))
            w = w + jnp.where(rank2_ref[h] < lr, e2_ref[h], jnp.zeros_like(w)) * e1
        wh.append(w * gh[al * PEER_NKEYS:(al + 1) * PEER_NKEYS])
    acc_ref[...] += jnp.dot(vt_ref[...], jnp.concatenate(wh, axis=0), preferred_element_type=F32)

    @pl.when(kb == pl.num_programs(1) - 1)
    def _():
        y_ref[...] = _layer_norm(DN_ALPHA * x_ref[...] + acc_ref[...].T, g_ref[...], b_ref[...])


def _peer(x, wq_bf, sk_bf, u_bf, vt_bf, g, b):
    n_tokens = x.shape[0]
    n = -(-n_tokens // LANES) * LANES
    x = jnp.pad(x, ((0, n - n_tokens), (0, 0)))
    tm = ROW_TILE if n % ROW_TILE == 0 else LANES
    n_exp = u_bf.shape[0]
    eb = PEER_EXPERT_BLOCK
    once = pl.Buffered(1)
    const2 = lambda a: pl.BlockSpec(a.shape, lambda i, k: (0, 0), pipeline_mode=once)
    return pl.pallas_call(
        _peer_kernel,
        grid=(n // tm, n_exp // eb),
        in_specs=[pl.BlockSpec((tm, D_MODEL), lambda i, k: (i, 0)),
                  const2(wq_bf),
                  pl.BlockSpec(sk_bf.shape, lambda i, k: (0, 0, 0, 0), pipeline_mode=once),
                  pl.BlockSpec((eb, D_MODEL), lambda i, k: (k, 0)),
                  pl.BlockSpec((D_MODEL, eb), lambda i, k: (0, k)),
                  const2(g), const2(b)],
        out_specs=pl.BlockSpec((tm, D_MODEL), lambda i, k: (i, 0)),
        out_shape=jax.ShapeDtypeStruct((n, D_MODEL), F32),
        scratch_shapes=[pltpu.VMEM((D_MODEL, tm), BF16),
                        pltpu.VMEM((PEER_HEADS, tm, 2 * PEER_KEY_DIM), BF16),
                        pltpu.VMEM((PEER_HEADS, PEER_NKEYS, tm), F32),
                        pltpu.VMEM((PEER_HEADS, PEER_NKEYS, tm), F32),
                        pltpu.VMEM((PEER_HEADS, PEER_NKEYS, tm), BF16),
                        pltpu.VMEM((PEER_HEADS, PEER_NKEYS, tm), BF16),
                        pltpu.VMEM((D_MODEL, tm), F32)],
        compiler_params=_params("parallel", "arbitrary"),
        name="peer_ln3",
    )(x, wq_bf, sk_bf, u_bf, vt_bf, g, b)[:n_tokens]


def kernel(x_prompt, x_sample, cache_k_cmp, cache_v_cmp, cache_k_slc, cache_v_slc, cache_k_win, cache_v_win,
           state_hgrn, cache_mem_k, cache_mem_v, page_table, mem_prompt, w_in, hgrn_lb_logits, hgrn_norm_g,
           cmp_k_w1, cmp_k_b1, cmp_k_w2, cmp_k_b2, cmp_v_w1, cmp_v_b1, cmp_v_w2, cmp_v_b2, w_out, ln1_g, ln1_b,
           w_mem_q, w_mem_k, w_mem_v, w_mem_o, ln2_g, ln2_b, peer_w_q, peer_sub_keys, peer_u, peer_v,
           ln3_g, ln3_b):
    bsz, seq, _ = x_prompt.shape
    db, dt, _ = x_sample.shape
    dtp = -(-dt // SUBLANES) * SUBLANES
    layer = 0

    w_in_pad = jnp.pad(w_in[layer], ((0, 0), (0, sum(IN_GROUP_WIDTHS) - IN_COLS))).astype(BF16)
    kv0 = 4 * HG_WIDTH + NSA_WIDTH
    w_in_main = jnp.concatenate([w_in_pad[:, :kv0 + 2 * KV_WIDTH], w_in_pad[:, kv0 + 6 * KV_WIDTH:]], axis=1)
    w_in_kv_t = w_in_pad[:, kv0:kv0 + 6 * KV_WIDTH].T
    kwts = _cmp_weights(cmp_k_w1[layer], cmp_k_b1[layer], cmp_k_w2[layer], cmp_k_b2[layer])
    vwts = _cmp_weights(cmp_v_w1[layer], cmp_v_b1[layer], cmp_v_w2[layer], cmp_v_b2[layer])
    w_out_bf = w_out[layer].astype(BF16)
    wmq, wmk, wmv, wmo = (w[layer].astype(BF16) for w in (w_mem_q, w_mem_k, w_mem_v, w_mem_o))
    pwq = peer_w_q[layer].astype(BF16)
    psk = peer_sub_keys[layer].astype(BF16)
    pu = peer_u[layer].astype(BF16)
    pvt = peer_v[layer].astype(BF16).T
    vec = lambda a: a[layer].reshape(1, D_MODEL)
    g1, b1, g2, b2, g3, b3 = (vec(a) for a in (ln1_g, ln1_b, ln2_g, ln2_b, ln3_g, ln3_b))
    norm_g = hgrn_norm_g[layer]

    n_p = bsz * seq
    xp = x_prompt.reshape(n_p, D_MODEL)
    zh, nq, kc, vc, ngate, kc_t, vc_t, ks_t, vs_t, kw_t, vw_t = _inproj_prompt(xp, w_in_main, w_in_kv_t, bsz, seq)
    per_b = lambda a: a.reshape(bsz, seq, a.shape[-1])
    chunk = min(HG_CHUNK, seq)
    o_h, s_p = _hgrn(per_b(zh), hgrn_lb_logits, norm_g, jnp.zeros((bsz, HG_HEADS, HG_D, HG_D), F32),
                     chunk=chunk, step_tokens=min(HG_STEP_TOKENS, seq), t_valid=chunk)
    kblk, vblk = _cmp_prompt(per_b(kc), per_b(vc), kwts, vwts)
    o_n = _nsa_prompt(per_b(nq), per_b(ngate), kblk, vblk, ks_t, vs_t, kw_t, vw_t)
    x1 = _outproj(o_h.reshape(n_p, HG_WIDTH), o_n.reshape(n_p, NSA_WIDTH), xp, w_out_bf, g1, b1)
    n_mem = mem_prompt.shape[1]
    mem_k, mem_v = _memkv(mem_prompt.reshape(bsz * n_mem, D_MODEL), wmk, wmv)
    mem_k = mem_k.reshape(bsz, n_mem, MEM_HEADS, MEM_HD)
    mem_v = mem_v.reshape(bsz, n_mem, MEM_HEADS, MEM_HD)
    x2 = _memattn(x1.reshape(bsz, seq, D_MODEL), mem_k, mem_v, wmq, wmo, g2, b2, nb=1, tm=min(ROW_TILE, seq))
    y_p = _peer(x2.reshape(n_p, D_MODEL), pwq, psk, pu, pvt, g3, b3).reshape(bsz, seq, D_MODEL)

    kv5 = lambda a: jnp.transpose(a.reshape(bsz, NSA_GROUPS, NSA_HD, a.shape[-1]), (0, 3, 1, 2))[None]
    n_win = min(WINDOW, seq)
    win5 = lambda a: kv5(a[:, :, seq - n_win:])
    mem5 = lambda a: a[None]

    n_s = db * dtp
    xs = jnp.pad(x_sample, ((0, 0), (0, dtp - dt), (0, 0))).reshape(n_s, D_MODEL)
    zh, nq, kc_s, vc_s, ks_s, vs_s, kw_s, vw_s, ngate = _inproj(xs, w_in_pad)
    per_s = lambda a: a.reshape(db, dtp, a.shape[-1])
    o_h, s_s = _hgrn(per_s(zh), hgrn_lb_logits, norm_g, state_hgrn[layer],
                     chunk=dtp, step_tokens=dtp, t_valid=dt)
    o_n = _nsa_sample(page_table, per_s(nq), per_s(ngate), per_s(ks_s), per_s(vs_s), per_s(kw_s), per_s(vw_s),
                      jnp.transpose(cache_k_win[layer], (0, 2, 3, 1)), jnp.transpose(cache_v_win[layer], (0, 2, 3, 1)),
                      kwts, vwts, cache_k_cmp[layer], cache_v_cmp[layer], cache_k_slc[layer], cache_v_slc[layer],
                      t_valid=dt)
    x1 = _outproj(o_h.reshape(n_s, HG_WIDTH), o_n.reshape(n_s, NSA_WIDTH), xs, w_out_bf, g1, b1)
    nb = min(4, db)
    x2 = _memattn(x1.reshape(db, dtp, D_MODEL), cache_mem_k[layer], cache_mem_v[layer],
                  wmq, wmo, g2, b2, nb=nb, tm=dtp)
    y_s = _peer(x2.reshape(n_s, D_MODEL), pwq, psk, pu, pvt, g3, b3).reshape(db, dtp, D_MODEL)[:, :dt]
    skv5 = lambda a: a.reshape(db, dtp, NSA_GROUPS, NSA_HD)[None, :, :dt]

    return (y_p, y_s, kv5(kc_t), kv5(vc_t), kv5(ks_t), kv5(vs_t), win5(kw_t), win5(vw_t), s_p[None],
            mem5(mem_k), mem5(mem_v),
            skv5(kc_s), skv5(vc_s), skv5(ks_s), skv5(vs_s), skv5(kw_s), skv5(vw_s), s_s[None])
```

```python
import functools

import numpy as np
import jax
import jax.numpy as jnp
from jax import lax
from jax.experimental import pallas as pl
from jax.experimental.pallas import tpu as pltpu

F32 = jnp.float32
BF16 = jnp.bfloat16

D_MODEL = 1024
HG_HEADS = 4
HG_D = 128
HG_WIDTH = HG_HEADS * HG_D
NSA_HEADS = 8
NSA_GROUPS = 2
NSA_HPG = NSA_HEADS // NSA_GROUPS
NSA_HD = 64
NSA_WIDTH = NSA_HEADS * NSA_HD
KV_WIDTH = NSA_GROUPS * NSA_HD
CMP_BLOCK = 32
CMP_STRIDE = 16
CMP_HIDDEN = 128
SEL_BLOCK = 64
N_SEL = 16
WINDOW = 512
PAGE_SIZE = 128
MEM_HEADS = 4
MEM_HD = D_MODEL // MEM_HEADS
PEER_HEADS = 8
PEER_NKEYS = 128
PEER_KEY_DIM = 128
PEER_TOPK = 16
DN_ALPHA = 2.0 ** 0.25
LN_EPS = 1e-5
NEG_INF = -1e30
LOG2E = 1.4426950408889634
ALIBI_SLOPES = tuple(2.0 ** (-8.0 * (h + 1) / NSA_HEADS) for h in range(NSA_HEADS))

LANES = 128
SUBLANES = 8
BF16_ROWS = 2 * SUBLANES
VMEM_LIMIT_BYTES = 56 * 1024 * 1024

HG_CHUNK = 32
HG_STEP_TOKENS = 256
HG_SEQS_PER_STEP = 2
NSA_Q_TILE = 128
ROW_TILE = 512
PEER_EXPERT_BLOCK = 2048
PEER_ROUTE_GROUP = 4
SLC_KEY_CHUNK = 512

IN_GROUP_WIDTHS = (4 * HG_WIDTH, NSA_WIDTH) + (KV_WIDTH,) * 6 + (LANES,)
IN_COLS = 4 * HG_WIDTH + NSA_WIDTH + 6 * KV_WIDTH + 3 * NSA_HEADS


def _params(*sem):
    return pltpu.CompilerParams(dimension_semantics=sem, vmem_limit_bytes=VMEM_LIMIT_BYTES)


def _gelu(x):
    return 0.5 * x * (1.0 + jnp.tanh(0.7978845608028654 * (x + 0.044715 * (x * x * x))))


def _layer_norm(y, g, b):
    mu = jnp.mean(y, -1, keepdims=True)
    yc = y - mu
    var = jnp.mean(yc * yc, -1, keepdims=True)
    return yc * lax.rsqrt(var + LN_EPS) * g + b


def _dot(a, b):
    return jnp.dot(a.astype(BF16), b.astype(BF16), preferred_element_type=F32)


def _dot_nt(a, b):
    return lax.dot_general(a.astype(BF16), b.astype(BF16), (((1,), (1,)), ((), ())),
                           preferred_element_type=F32)


def _dot_tn(a, b):
    return lax.dot_general(a.astype(BF16), b.astype(BF16), (((0,), (0,)), ((), ())),
                           preferred_element_type=F32)


def _masked_exp(s, mask):
    sm = jnp.where(mask, s, NEG_INF)
    m = jnp.max(sm, -1, keepdims=True)
    e = jnp.where(mask, jnp.exp(sm - m), 0.0)
    return e, jnp.sum(e, -1, keepdims=True)


def _safe_inv(d):
    return jnp.where(d > 0.0, 1.0 / jnp.where(d > 0.0, d, 1.0), 0.0)


def _inproj_kernel(x_ref, w_ref, *out_refs):
    xb = x_ref[...].astype(BF16)
    off = 0
    for o_ref in out_refs:
        width = o_ref.shape[-1]
        o_ref[...] = jnp.dot(xb, w_ref[:, off:off + width], preferred_element_type=F32)
        off += width


def _inproj(x, w_pad):
    n = x.shape[0]
    tm = min(ROW_TILE, n)
    return pl.pallas_call(
        _inproj_kernel,
        grid=(n // tm,),
        in_specs=[pl.BlockSpec((tm, D_MODEL), lambda i: (i, 0)),
                  pl.BlockSpec(w_pad.shape, lambda i: (0, 0))],
        out_specs=[pl.BlockSpec((tm, w), lambda i: (i, 0)) for w in IN_GROUP_WIDTHS],
        out_shape=[jax.ShapeDtypeStruct((n, w), F32) for w in IN_GROUP_WIDTHS],
        compiler_params=_params("parallel"),
        name="inproj",
    )(x, w_pad)


def _inproj_prompt_kernel(x_ref, w_ref, wkv_t_ref, zh_ref, nq_ref, kc_ref, vc_ref, ng_ref, *kv_t_refs):
    xb = x_ref[...].astype(BF16)
    off = 0
    for o_ref in (zh_ref, nq_ref, kc_ref, vc_ref, ng_ref):
        width = o_ref.shape[-1]
        o_ref[...] = jnp.dot(xb, w_ref[:, off:off + width], preferred_element_type=F32)
        off += width
    z_t = lax.dot_general(wkv_t_ref[...], xb, (((1,), (1,)), ((), ())), preferred_element_type=F32)
    for i, o_ref in enumerate(kv_t_refs):
        o_ref[0] = z_t[i * KV_WIDTH:(i + 1) * KV_WIDTH]


def _inproj_prompt(x, w_main, wkv_t, bsz, seq):
    n = x.shape[0]
    tm = min(ROW_TILE, seq)
    tiles = seq // tm
    widths = (4 * HG_WIDTH, NSA_WIDTH, KV_WIDTH, KV_WIDTH, LANES)
    n_kv = wkv_t.shape[0] // KV_WIDTH
    return pl.pallas_call(
        _inproj_prompt_kernel,
        grid=(n // tm,),
        in_specs=[pl.BlockSpec((tm, D_MODEL), lambda i: (i, 0)),
                  pl.BlockSpec(w_main.shape, lambda i: (0, 0)),
                  pl.BlockSpec(wkv_t.shape, lambda i: (0, 0))],
        out_specs=[pl.BlockSpec((tm, w), lambda i: (i, 0)) for w in widths]
                  + [pl.BlockSpec((1, KV_WIDTH, tm), lambda i: (i // tiles, 0, i % tiles))] * n_kv,
        out_shape=[jax.ShapeDtypeStruct((n, w), F32) for w in widths]
                  + [jax.ShapeDtypeStruct((bsz, KV_WIDTH, seq), F32)] * n_kv,
        compiler_params=_params("parallel"),
        name="inproj_prompt",
    )(x, w_main, wkv_t)


def _hgrn_kernel(zh_ref, lbl_ref, ng_ref, s0_ref, tri_ref, o_ref, sout_ref, st_ref, obuf_ref,
                 *, chunk, t_valid):
    ti = pl.program_id(1)
    n_seq, step_tokens, _ = zh_ref.shape
    units = [(s, h) for s in range(n_seq) for h in range(HG_HEADS)]

    @pl.when(ti == 0)
    def _():
        for s, h in units:
            st_ref[s, h] = s0_ref[s, h].T

    logits = lbl_ref[...]
    le = jnp.exp(logits - jnp.max(logits, axis=0, keepdims=True))
    lb_all = le[0:1] / jnp.sum(le, axis=0, keepdims=True)
    norm_g = ng_ref[...]
    row = lax.broadcasted_iota(jnp.int32, (chunk, 1), 0)
    row8 = lax.broadcasted_iota(jnp.int32, (SUBLANES, 1), 0)
    n_tiles = chunk // SUBLANES
    tile_match = (lax.broadcasted_iota(jnp.int32, (chunk, n_tiles * chunk), 0) // SUBLANES
                  == lax.broadcasted_iota(jnp.int32, (chunk, n_tiles * chunk), 1) // chunk)

    def do_chunk(c, carry):
        r0 = pl.multiple_of(c * chunk, chunk)
        rows = pl.ds(r0, chunk)
        for s, h in units:
            cs = h * HG_D
            q = zh_ref[s, rows, cs:cs + HG_D]
            zf = zh_ref[s, rows, HG_WIDTH + cs:HG_WIDTH + cs + HG_D]
            v = zh_ref[s, rows, 2 * HG_WIDTH + cs:2 * HG_WIDTH + cs + HG_D]
            zg = zh_ref[s, rows, 3 * HG_WIDTH + cs:3 * HG_WIDTH + cs + HG_D]
            lb = lb_all[:, cs:cs + HG_D]
            lf = jnp.log(lb + (1.0 - lb) * jax.nn.sigmoid(zf))
            k = (1.0 - lb) * jax.nn.sigmoid(-zf)
            if t_valid < chunk:
                lf = jnp.where(row < t_valid, lf, 0.0)
                k = jnp.where(row < t_valid, k, 0.0)
            if chunk <= SUBLANES:
                g = jnp.where(row >= 0, lf[0:1], 0.0)
                for t in range(1, chunk):
                    g = g + jnp.where(row >= t, lf[t:t + 1], 0.0)
            else:
                g = jnp.dot(tri_ref[...], lf, precision=lax.Precision.HIGHEST, preferred_element_type=F32)
            st = st_ref[s, h]
            o = _dot_nt(q * jnp.exp(g), st)
            for t in range(chunk):
                r0 = t // SUBLANES * SUBLANES
                rel = jnp.where(row8 <= t - r0, g[t:t + 1] - g[r0:r0 + SUBLANES], NEG_INF)
                p = (q[t:t + 1] * k[r0:r0 + SUBLANES]) * jnp.exp(rel)
                w = jnp.sum(p, axis=-1, keepdims=True)
                obuf_ref[s, h, t:t + 1, :] = jnp.sum(w * v[r0:r0 + SUBLANES], axis=0, keepdims=True)
            o = o + obuf_ref[s, h]
            if n_tiles > 1:
                g_ref_rows = jnp.concatenate(
                    [jnp.broadcast_to(g[max(i * SUBLANES - 1, 0):max(i * SUBLANES - 1, 0) + 1], (SUBLANES, HG_D))
                     for i in range(n_tiles)], axis=0)
                qa = q * jnp.exp(g - g_ref_rows)
                ka = [jnp.zeros_like(k)]
                for i in range(1, n_tiles):
                    shift = jnp.where(row < i * SUBLANES, g[i * SUBLANES - 1:i * SUBLANES] - g, NEG_INF)
                    ka.append(k * jnp.exp(shift))
                att = _dot_nt(qa, jnp.concatenate(ka, axis=0))
                att = jnp.where(tile_match, att, 0.0)
                o = o + _dot(att, jnp.concatenate([v] * n_tiles, axis=0))
            g_end = g[chunk - 1:chunk]
            kk = k * jnp.exp(g_end - g)
            st_ref[s, h] = st * jnp.exp(g_end) + _dot_tn(v, kk)
            o = o * lax.rsqrt(jnp.mean(o * o, -1, keepdims=True) + LN_EPS) * norm_g
            o_ref[s, rows, cs:cs + HG_D] = o * jax.nn.sigmoid(zg)
        return carry

    lax.fori_loop(0, step_tokens // chunk, do_chunk, 0)

    @pl.when(ti == pl.num_programs(1) - 1)
    def _():
        for s, h in units:
            sout_ref[s, h] = st_ref[s, h].T


def _hgrn(zh, lb_logits, norm_g, s0, *, chunk, step_tokens, t_valid):
    b, t, _ = zh.shape
    n_seq = HG_SEQS_PER_STEP if b % HG_SEQS_PER_STEP == 0 else 1
    tri = jnp.asarray(np.tril(np.ones((chunk, chunk), np.float32)))
    kern = functools.partial(_hgrn_kernel, chunk=chunk, t_valid=t_valid)
    return pl.pallas_call(
        kern,
        grid=(b // n_seq, t // step_tokens),
        in_specs=[pl.BlockSpec((n_seq, step_tokens, 4 * HG_WIDTH), lambda i, j: (i, j, 0)),
                  pl.BlockSpec(lb_logits.shape, lambda i, j: (0, 0)),
                  pl.BlockSpec((1, HG_D), lambda i, j: (0, 0)),
                  pl.BlockSpec((n_seq, HG_HEADS, HG_D, HG_D), lambda i, j: (i, 0, 0, 0)),
                  pl.BlockSpec((chunk, chunk), lambda i, j: (0, 0))],
        out_specs=[pl.BlockSpec((n_seq, step_tokens, HG_WIDTH), lambda i, j: (i, j, 0)),
                   pl.BlockSpec((n_seq, HG_HEADS, HG_D, HG_D), lambda i, j: (i, 0, 0, 0))],
        out_shape=[jax.ShapeDtypeStruct((b, t, HG_WIDTH), F32),
                   jax.ShapeDtypeStruct((b, HG_HEADS, HG_D, HG_D), F32)],
        scratch_shapes=[pltpu.VMEM((n_seq, HG_HEADS, HG_D, HG_D), F32),
                        pltpu.VMEM((n_seq, HG_HEADS, chunk, HG_D), F32)],
        compiler_params=_params("parallel", "arbitrary"),
        name="hgrn",
    )(zh, lb_logits, norm_g.reshape(1, HG_D), s0, tri)


def _cmp_blocks(view, w1_ref, b1_ref, w2_ref, b2_ref):
    return _cmp_finish(_dot(view, w1_ref[...]), b1_ref, w2_ref, b2_ref)


def _cmp_blocks_from_rows(rows_ref, w1_ref, b1_ref, w2_ref, b2_ref):
    n_sub = rows_ref.shape[0] // CMP_STRIDE
    a = None
    for p in range(CMP_STRIDE):
        part = _dot(rows_ref[pl.ds(p, n_sub, stride=CMP_STRIDE), :], w1_ref[p * KV_WIDTH:(p + 1) * KV_WIDTH, :])
        a = part if a is None else a + part
    return _cmp_finish(a, b1_ref, w2_ref, b2_ref)


def _cmp_finish(a, b1_ref, w2_ref, b2_ref):
    n_sub = a.shape[0]
    hid = []
    for g in range(NSA_GROUPS):
        a0 = a[:, (2 * g) * CMP_HIDDEN:(2 * g + 1) * CMP_HIDDEN]
        a1 = a[:, (2 * g + 1) * CMP_HIDDEN:(2 * g + 2) * CMP_HIDDEN]
        hid.append(a0 + pltpu.roll(a1, n_sub - 1, axis=0))
    h = jnp.concatenate(hid, axis=-1) + b1_ref[...]
    return _dot(_gelu(h), w2_ref[...]) + b2_ref[...]


def _cmp_kernel(kv_ref, vv_ref, kw1, kb1, kw2, kb2, vw1, vb1, vw2, vb2, ko_ref, vo_ref):
    ko_ref[0] = _cmp_blocks(kv_ref[0], kw1, kb1, kw2, kb2)
    vo_ref[0] = _cmp_blocks(vv_ref[0], vw1, vb1, vw2, vb2)


def _cmp_weights(w1, b1, w2, b2):
    w1r = w1.reshape(CMP_BLOCK // CMP_STRIDE, CMP_STRIDE, NSA_HD, CMP_HIDDEN)
    big = jnp.zeros((CMP_STRIDE, NSA_GROUPS, NSA_HD, NSA_GROUPS, 2, CMP_HIDDEN), F32)
    for g in range(NSA_GROUPS):
        big = big.at[:, g, :, g, :, :].set(jnp.transpose(w1r, (1, 2, 0, 3)))
    w1b = big.reshape(CMP_STRIDE * KV_WIDTH, NSA_GROUPS * 2 * CMP_HIDDEN).astype(BF16)
    w2b = jnp.zeros((NSA_GROUPS, CMP_HIDDEN, NSA_GROUPS, NSA_HD), F32)
    for g in range(NSA_GROUPS):
        w2b = w2b.at[g, :, g, :].set(w2)
    w2b = w2b.reshape(NSA_GROUPS * CMP_HIDDEN, KV_WIDTH).astype(BF16)
    b1b = jnp.tile(b1, NSA_GROUPS).reshape(1, NSA_GROUPS * CMP_HIDDEN)
    b2b = jnp.tile(b2, NSA_GROUPS).reshape(1, KV_WIDTH)
    return w1b, b1b, w2b, b2b


def _cmp_prompt(kc, vc, kwts, vwts):
    b, t, _ = kc.shape
    n_sub = t // CMP_STRIDE
    view = lambda a: a.reshape(b, n_sub, CMP_STRIDE * KV_WIDTH)
    wspecs = [pl.BlockSpec(w.shape, lambda i: (0, 0)) for w in kwts + vwts]
    return pl.pallas_call(
        _cmp_kernel,
        grid=(b,),
        in_specs=[pl.BlockSpec((1, n_sub, CMP_STRIDE * KV_WIDTH), lambda i: (i, 0, 0))] * 2 + wspecs,
        out_specs=[pl.BlockSpec((1, n_sub, KV_WIDTH), lambda i: (i, 0, 0))] * 2,
        out_shape=[jax.ShapeDtypeStruct((b, n_sub, KV_WIDTH), F32)] * 2,
        compiler_params=_params("parallel"),
        name="cmp_prompt",
    )(view(kc), view(vc), *kwts, *vwts)


def _cmp_to_slc(n_cmp, n_slc):
    cs = np.arange(n_cmp)[:, None] * CMP_STRIDE
    ss = np.arange(n_slc)[None, :] * SEL_BLOCK
    shared = np.clip(np.minimum(cs + CMP_BLOCK, ss + SEL_BLOCK) - np.maximum(cs, ss), 0, None)
    return (shared / CMP_STRIDE).astype(np.float32)


def _softmax2(s):
    m = jnp.max(s, -1, keepdims=True)
    e = jnp.exp2(s - m)
    inv = jnp.where(m > 0.5 * NEG_INF, 1.0 / jnp.sum(e, -1, keepdims=True), 0.0)
    return e, inv


def _nsa_prompt_kernel(q_ref, gate_ref, kb_ref, vb_ref, ks_ref, vs_ref, kw_ref, vw_ref, mt_ref, ex_ref,
                       o_ref, bias_ref, *, n_cmp, n_var):
    tq = q_ref.shape[1]
    t_all = ks_ref.shape[2]
    n_sub = kb_ref.shape[1]
    n_slc = mt_ref.shape[0]
    tile = pl.program_id(1)
    t0 = tile * tq
    q = q_ref[0] * (NSA_HD ** -0.5 * LOG2E)
    slopes = tuple(s * LOG2E for s in ALIBI_SLOPES)
    gates = jax.nn.sigmoid(gate_ref[0])
    qpos = t0 + lax.broadcasted_iota(jnp.int32, (tq, 1), 0)

    def stack_heads(g):
        return jnp.concatenate([q[:, h * NSA_HD:(h + 1) * NSA_HD] for h in range(g * NSA_HPG, (g + 1) * NSA_HPG)],
                               axis=0)

    cidx = lax.broadcasted_iota(jnp.int32, (1, n_sub), 1)
    kend =cidx * CMP_STRIDE + CMP_BLOCK - 1
    bias_c = jnp.where((qpos - kend >= 0) & (cidx < n_cmp), 0.0, NEG_INF)
    kend_rel = (kend - t0).astype(F32)
    nidx = lax.broadcasted_iota(jnp.int32, (n_slc, 1), 0)
    tpos = t0 + lax.broadcasted_iota(jnp.int32, (1, tq), 1)
    elig = nidx * SEL_BLOCK <= tpos
    force = (nidx == tpos // SEL_BLOCK) | (nidx == 0)
    causal = qpos - lax.broadcasted_iota(jnp.int32, (1, t_all), 1) >= 0
    span = min(WINDOW + tq, t_all)
    start = pl.multiple_of(jnp.clip(t0 - WINDOW, 0, t_all - span), tq)
    kpos_w = start + lax.broadcasted_iota(jnp.int32, (1, span), 1)
    bias_w = jnp.where((qpos - kpos_w >= 0) & (qpos - kpos_w < WINDOW), 0.0, NEG_INF)
    kpos_w_rel = (kpos_w - t0).astype(F32)

    for g in range(NSA_GROUPS):
        gs = g * NSA_HD
        kb = kb_ref[0, :, gs:gs + NSA_HD]
        vb = vb_ref[0, :, gs:gs + NSA_HD]
        kw = kw_ref[0, gs:gs + NSA_HD, pl.ds(start, span)]
        vw = vw_ref[0, gs:gs + NSA_HD, pl.ds(start, span)]
        heads = range(g * NSA_HPG, (g + 1) * NSA_HPG)
        qg = stack_heads(g)
        e, inv = _softmax2(_dot_nt(qg, kb) + jnp.concatenate([bias_c + slopes[h] * kend_rel for h in heads], axis=0))
        p = e * inv
        o_cmp = _dot(p, vb)
        psum = p[0:tq]
        for j in range(1, NSA_HPG):
            psum = psum + p[j * tq:(j + 1) * tq]
        e, inv = _softmax2(_dot(qg, kw) + jnp.concatenate([bias_w + slopes[h] * kpos_w_rel for h in heads], axis=0))
        o_win = _dot_nt(e, vw) * inv
        for j, h in enumerate(heads):
            rs = slice(j * tq, (j + 1) * tq)
            o_ref[0, :, h * NSA_HD:(h + 1) * NSA_HD] = (
                gates[:, 3 * h:3 * h + 1] * o_cmp[rs] + gates[:, 3 * h + 2:3 * h + 3] * o_win[rs])
        imp_t = lax.dot_general(mt_ref[...], psum, (((1,), (1,)), ((), ())),
                                precision=lax.Precision.HIGHEST, preferred_element_type=F32)
        val = jnp.where(force, jnp.inf, jnp.where(elig, imp_t, -jnp.inf))
        cnt = jnp.zeros((n_slc, tq), F32)
        for m in range(n_slc):
            other = val[m:m + 1]
            tie = jnp.where(nidx > m, 1.0, 0.0)
            cnt = cnt + jnp.where(other > val, 1.0, jnp.where(other == val, tie, 0.0))
        sel_t = jnp.where(cnt < float(min(N_SEL, n_slc)), 1.0, 0.0)
        sel_keys = _dot(sel_t.T, ex_ref[...])
        bias_ref[g] = jnp.where((sel_keys > 0.5) & causal, 0.0, NEG_INF)

    tiles_per_var = (t_all // tq) // n_var
    for v in range(n_var):
        n_keys = (v + 1) * tiles_per_var * tq

        @pl.when(tile // tiles_per_var == v)
        def _(n_keys=n_keys):
            kpos_rel = (lax.broadcasted_iota(jnp.int32, (1, n_keys), 1) - t0).astype(F32)
            for g in range(NSA_GROUPS):
                gs = g * NSA_HD
                ks = ks_ref[0, gs:gs + NSA_HD, 0:n_keys]
                vs = vs_ref[0, gs:gs + NSA_HD, 0:n_keys]
                bias_s = bias_ref[g, :, 0:n_keys]
                heads = range(g * NSA_HPG, (g + 1) * NSA_HPG)
                bias = jnp.concatenate([bias_s + slopes[h] * kpos_rel for h in heads], axis=0)
                e, inv = _softmax2(_dot(stack_heads(g), ks) + bias)
                o_slc = _dot_nt(e, vs) * inv
                for j, h in enumerate(heads):
                    o_ref[0, :, h * NSA_HD:(h + 1) * NSA_HD] += gates[:, 3 * h + 1:3 * h + 2] * o_slc[j * tq:(j + 1) * tq]


def _nsa_prompt(nq, ng, kblk, vblk, ks, vs, kw, vw):
    b, t, _ = nq.shape
    tq = min(NSA_Q_TILE, t)
    n_sub = kblk.shape[1]
    n_cmp = n_sub - CMP_BLOCK // CMP_STRIDE + 1
    n_slc = -(-t // SEL_BLOCK)
    n_var = max(1, t // SLC_KEY_CHUNK)
    assert (t // tq) % n_var == 0
    mt = np.zeros((n_slc, n_sub), np.float32)
    mt[:, :n_cmp] = _cmp_to_slc(n_cmp, n_slc).T
    expand = (np.arange(t)[None, :] // SEL_BLOCK == np.arange(n_slc)[:, None]).astype(np.float32)
    full = lambda w: pl.BlockSpec((1, w, t), lambda i, j: (i, 0, 0))
    return pl.pallas_call(
        functools.partial(_nsa_prompt_kernel, n_cmp=n_cmp, n_var=n_var),
        grid=(b, t // tq),
        in_specs=[pl.BlockSpec((1, tq, NSA_WIDTH), lambda i, j: (i, j, 0)),
                  pl.BlockSpec((1, tq, LANES), lambda i, j: (i, j, 0)),
                  pl.BlockSpec((1, n_sub, KV_WIDTH), lambda i, j: (i, 0, 0)),
                  pl.BlockSpec((1, n_sub, KV_WIDTH), lambda i, j: (i, 0, 0)),
                  full(KV_WIDTH), full(KV_WIDTH), full(KV_WIDTH), full(KV_WIDTH),
                  pl.BlockSpec(mt.shape, lambda i, j: (0, 0)),
                  pl.BlockSpec(expand.shape, lambda i, j: (0, 0))],
        out_specs=pl.BlockSpec((1, tq, NSA_WIDTH), lambda i, j: (i, j, 0)),
        out_shape=jax.ShapeDtypeStruct((b, t, NSA_WIDTH), F32),
        scratch_shapes=[pltpu.VMEM((NSA_GROUPS, tq, t), F32)],
        compiler_params=_params("parallel", "arbitrary"),
        name="nsa_prompt",
    )(nq, ng, kblk, vblk, ks, vs, kw, vw, jnp.asarray(mt), jnp.asarray(expand, dtype=BF16))


def _nsa_sample_kernel(pt_ref, q_ref, gate_ref, ksn_ref, vsn_ref, kwn_ref, vwn_ref, wk_ref, wv_ref,
                       kw1, kb1, kw2, kb2, vw1, vb1, vw2, vb2, m_ref, ex_ref,
                       kc_pool, vc_pool, ks_pool, vs_pool,
                       o_ref, kc_buf, vc_buf, ks_buf, vs_buf, sem, *, t_valid, n_slc):
    b = pl.program_id(0)
    n_pages = pt_ref.shape[1]
    tp = q_ref.shape[1]
    past = n_pages * PAGE_SIZE
    n_sub = past // CMP_STRIDE
    n_cmp = n_sub - CMP_BLOCK // CMP_STRIDE + 1
    n_past_blocks = past // SEL_BLOCK

    pools = ((kc_pool, kc_buf, False), (vc_pool, vc_buf, False), (ks_pool, ks_buf, True), (vs_pool, vs_buf, True))
    slot = b % 2

    def page_copy(seq, s, i, j):
        pool, buf, rows_on_lanes = pools[i]
        rows = pl.ds(pl.multiple_of(j * PAGE_SIZE, PAGE_SIZE), PAGE_SIZE)
        dst = buf.at[s, :, :, rows] if rows_on_lanes else buf.at[s, rows]
        return pltpu.make_async_copy(pool.at[pt_ref[seq, j]], dst, sem.at[s, i])

    def start_pages(seq, s):
        def body(j, carry):
            for i in range(len(pools)):
                page_copy(seq, s, i, j).start()
            return carry
        lax.fori_loop(0, n_pages, body, 0)

    def wait_pages(which):
        def body(j, carry):
            for i in which:
                page_copy(b, slot, i, j).wait()
            return carry
        lax.fori_loop(0, n_pages, body, 0)

    @pl.when(b == 0)
    def _():
        start_pages(b, slot)

    @pl.when(b + 1 < pl.num_programs(0))
    def _():
        start_pages(b + 1, 1 - slot)

    wait_pages((0, 1))
    kblk = _cmp_blocks_from_rows(kc_buf.at[slot], kw1, kb1, kw2, kb2)
    vblk = _cmp_blocks_from_rows(vc_buf.at[slot], vw1, vb1, vw2, vb2)
    wait_pages((2, 3))

    q = q_ref[0] * (NSA_HD ** -0.5)
    gates = jax.nn.sigmoid(gate_ref[0])
    trow = lax.broadcasted_iota(jnp.int32, (tp, 1), 0)
    trow_q = jnp.concatenate([trow] * NSA_HPG, axis=0)
    qpos_q = past + trow_q
    cidx = lax.broadcasted_iota(jnp.int32, (1, n_sub), 1)
    dist_c = qpos_q - (cidx * CMP_STRIDE + CMP_BLOCK - 1)
    mask_c = (dist_c >= 0) & (cidx < n_cmp)
    dist_cf = dist_c.astype(F32)
    n_rows = m_ref.shape[0]
    nidx = lax.broadcasted_iota(jnp.int32, (n_rows, 1), 0)
    nidx_f = nidx.astype(F32)
    qpos_t = past + lax.broadcasted_iota(jnp.int32, (1, tp), 1)
    elig = (nidx * SEL_BLOCK <= qpos_t) & (nidx < n_slc)
    force = ((nidx == qpos_t // SEL_BLOCK) | (nidx == 0)) & (nidx < n_slc)
    eye = lax.broadcasted_iota(jnp.int32, (tp, tp), 0) == lax.broadcasted_iota(jnp.int32, (tp, tp), 1)
    kpos = lax.broadcasted_iota(jnp.int32, (1, past), 1)
    dist_pf = (qpos_q - kpos).astype(F32)
    rnew = lax.broadcasted_iota(jnp.int32, (1, tp), 1)
    dist_n = trow_q - rnew
    mask_n = (dist_n >= 0) & (rnew < t_valid)
    dist_nf = dist_n.astype(F32)
    n_buf = wk_ref.shape[3]
    dist_b = qpos_q - (past - n_buf + lax.broadcasted_iota(jnp.int32, (1, n_buf), 1))
    mask_b = (dist_b >= 0) & (dist_b < WINDOW)
    dist_bf = dist_b.astype(F32)

    for g in range(NSA_GROUPS):
        gs = g * NSA_HD
        heads = range(g * NSA_HPG, (g + 1) * NSA_HPG)
        qg = jnp.concatenate([q[:, h * NSA_HD:(h + 1) * NSA_HD] for h in heads], axis=0)
        slope = jnp.concatenate([jnp.full((tp, 1), ALIBI_SLOPES[h], F32) for h in heads], axis=0)
        s = _dot_nt(qg, kblk[:, gs:gs + NSA_HD]) - slope * dist_cf
        e, d = _masked_exp(s, mask_c)
        p = e * _safe_inv(d)
        o_cmp = _dot(p, vblk[:, gs:gs + NSA_HD])
        psum = p[0:tp]
        for j in range(1, NSA_HPG):
            psum = psum + p[j * tp:(j + 1) * tp]
        imp = lax.dot_general(m_ref[...], psum, (((1,), (1,)), ((), ())),
                              precision=lax.Precision.HIGHEST, preferred_element_type=F32)
        val = jnp.where(force, jnp.inf, jnp.where(elig, imp, -jnp.inf))
        sel = jnp.zeros((n_rows, tp), F32)
        for _ in range(min(N_SEL, n_slc)):
            m = jnp.max(val, axis=0, keepdims=True)
            first = jnp.min(jnp.where(val == m, nidx_f, float(n_rows)), axis=0, keepdims=True)
            pick = nidx_f == first
            sel = jnp.where(pick, 1.0, sel)
            val = jnp.where(pick, -jnp.inf, val)
        sel = jnp.where(nidx < n_slc, sel, 0.0)
        sel_keys = _dot_tn(sel[:n_past_blocks], ex_ref[...])
        mask_p = jnp.concatenate([sel_keys] * NSA_HPG, axis=0) > 0.5
        sel_new = jnp.sum(jnp.where(eye, sel[n_past_blocks:n_past_blocks + 1], 0.0), axis=-1, keepdims=True)
        sel_new = jnp.concatenate([sel_new] * NSA_HPG, axis=0) > 0.5
        s_p = _dot(qg, ks_buf[slot, g]) - slope * dist_pf
        s_n = _dot_nt(qg, ksn_ref[0, :, gs:gs + NSA_HD]) - slope * dist_nf
        mask_sn = mask_n & sel_new
        mx = jnp.maximum(jnp.max(jnp.where(mask_p, s_p, NEG_INF), -1, keepdims=True),
                         jnp.max(jnp.where(mask_sn, s_n, NEG_INF), -1, keepdims=True))
        e_p = jnp.where(mask_p, jnp.exp(jnp.where(mask_p, s_p, NEG_INF) - mx), 0.0)
        e_n = jnp.where(mask_sn, jnp.exp(jnp.where(mask_sn, s_n, NEG_INF) - mx), 0.0)
        den = jnp.sum(e_p, -1, keepdims=True) + jnp.sum(e_n, -1, keepdims=True)
        o_slc = (_dot_nt(e_p, vs_buf[slot, g]) + _dot(e_n, vsn_ref[0, :, gs:gs + NSA_HD])) * _safe_inv(den)
        s_b = _dot(qg, wk_ref[0, g]) - slope * dist_bf
        s_n = _dot_nt(qg, kwn_ref[0, :, gs:gs + NSA_HD]) - slope * dist_nf
        mx = jnp.maximum(jnp.max(jnp.where(mask_b, s_b, NEG_INF), -1, keepdims=True),
                         jnp.max(jnp.where(mask_n, s_n, NEG_INF), -1, keepdims=True))
        e_b = jnp.where(mask_b, jnp.exp(jnp.where(mask_b, s_b, NEG_INF) - mx), 0.0)
        e_n = jnp.where(mask_n, jnp.exp(jnp.where(mask_n, s_n, NEG_INF) - mx), 0.0)
        den = jnp.sum(e_b, -1, keepdims=True) + jnp.sum(e_n, -1, keepdims=True)
        o_win = (_dot_nt(e_b, wv_ref[0, g]) + _dot(e_n, vwn_ref[0, :, gs:gs + NSA_HD])) * _safe_inv(den)
        for j, h in enumerate(heads):
            rs = slice(j * tp, (j + 1) * tp)
            o_ref[0, :, h * NSA_HD:(h + 1) * NSA_HD] = (
                gates[:, 3 * h:3 * h + 1] * o_cmp[rs]
                + gates[:, 3 * h + 1:3 * h + 2] * o_slc[rs]
                + gates[:, 3 * h + 2:3 * h + 3] * o_win[rs])


def _nsa_sample(page_table, nq, ng, ksn, vsn, kwn, vwn, win_k, win_v, kwts, vwts,
                pool_kc, pool_vc, pool_ks, pool_vs, *, t_valid):
    db, tp, _ = nq.shape
    n_pages = page_table.shape[1]
    past = n_pages * PAGE_SIZE
    n_sub = past // CMP_STRIDE
    n_cmp = n_sub - CMP_BLOCK // CMP_STRIDE + 1
    n_slc = -(-(past + t_valid) // SEL_BLOCK)
    n_rows = -(-n_slc // SUBLANES) * SUBLANES
    n_buf = win_k.shape[3]
    m = np.zeros((n_rows, n_sub), np.float32)
    m[:n_slc, :n_cmp] = _cmp_to_slc(n_cmp, n_slc).T
    n_past_blocks = past // SEL_BLOCK
    expand = (np.arange(past)[None, :] // SEL_BLOCK == np.arange(n_past_blocks)[:, None]).astype(np.float32)
    row_view = lambda p: p.reshape(p.shape[0], PAGE_SIZE, KV_WIDTH)
    col_view = lambda p: jnp.transpose(p, (0, 2, 3, 1))
    per_b = lambda r, w: pl.BlockSpec((1, r, w), lambda i, pt: (i, 0, 0))
    const = lambda a: pl.BlockSpec(a.shape, lambda i, pt: (0,) * a.ndim, pipeline_mode=pl.Buffered(1))
    wts = kwts + vwts
    m_j = jnp.asarray(m)
    ex_j = jnp.asarray(expand, dtype=BF16)
    any_spec = pl.BlockSpec(memory_space=pl.ANY)
    grid_spec = pltpu.PrefetchScalarGridSpec(
        num_scalar_prefetch=1,
        grid=(db,),
        in_specs=[per_b(tp, NSA_WIDTH), per_b(tp, LANES)] + [per_b(tp, KV_WIDTH)] * 4
                 + [pl.BlockSpec((1, NSA_GROUPS, NSA_HD, n_buf), lambda i, pt: (i, 0, 0, 0))] * 2
                 + [const(w) for w in wts] + [const(m_j), const(ex_j)]
                 + [any_spec] * 4,
        out_specs=per_b(tp, NSA_WIDTH),
        scratch_shapes=[pltpu.VMEM((2, past, KV_WIDTH), F32)] * 2
                       + [pltpu.VMEM((2, NSA_GROUPS, NSA_HD, past), F32)] * 2
                       + [pltpu.SemaphoreType.DMA((2, 4))],
    )
    return pl.pallas_call(
        functools.partial(_nsa_sample_kernel, t_valid=t_valid, n_slc=n_slc),
        grid_spec=grid_spec,
        out_shape=jax.ShapeDtypeStruct((db, tp, NSA_WIDTH), F32),
        compiler_params=_params("arbitrary"),
        name="nsa_sample",
    )(page_table, nq, ng, ksn, vsn, kwn, vwn, win_k, win_v, *wts, m_j, ex_j,
      row_view(pool_kc), row_view(pool_vc), col_view(pool_ks), col_view(pool_vs))


def _outproj_kernel(oh_ref, on_ref, x_ref, w_ref, g_ref, b_ref, y_ref):
    mix = (jnp.dot(oh_ref[...].astype(BF16), w_ref[:HG_WIDTH, :], preferred_element_type=F32)
           + jnp.dot(on_ref[...].astype(BF16), w_ref[HG_WIDTH:, :], preferred_element_type=F32))
    y_ref[...] = _layer_norm(DN_ALPHA * x_ref[...] + mix, g_ref[...], b_ref[...])


def _outproj(o_h, o_n, x, w_bf, g, b):
    n = x.shape[0]
    tm = min(ROW_TILE, n)
    row = lambda w: pl.BlockSpec((tm, w), lambda i: (i, 0))
    const = lambda a: pl.BlockSpec(a.shape, lambda i: (0, 0))
    return pl.pallas_call(
        _outproj_kernel,
        grid=(n // tm,),
        in_specs=[row(HG_WIDTH), row(NSA_WIDTH), row(D_MODEL), const(w_bf), const(g), const(b)],
        out_specs=row(D_MODEL),
        out_shape=jax.ShapeDtypeStruct((n, D_MODEL), F32),
        compiler_params=_params("parallel"),
        name="outproj_ln1",
    )(o_h, o_n, x, w_bf, g, b)


def _memkv_kernel(m_ref, wk_ref, wv_ref, k_ref, v_ref):
    mb = m_ref[...].astype(BF16)
    for w_ref, o_ref in ((wk_ref, k_ref), (wv_ref, v_ref)):
        res = jnp.dot(mb, w_ref[...], preferred_element_type=F32)
        for h in range(MEM_HEADS):
            o_ref[:, h, :] = res[:, h * MEM_HD:(h + 1) * MEM_HD]


def _memkv(mem, wk_bf, wv_bf):
    n = mem.shape[0]
    tm = min(ROW_TILE, n)
    row = pl.BlockSpec((tm, D_MODEL), lambda i: (i, 0))
    heads = pl.BlockSpec((tm, MEM_HEADS, MEM_HD), lambda i: (i, 0, 0))
    const = lambda a: pl.BlockSpec(a.shape, lambda i: (0, 0))
    return pl.pallas_call(
        _memkv_kernel,
        grid=(n // tm,),
        in_specs=[row, const(wk_bf), const(wv_bf)],
        out_specs=[heads, heads],
        out_shape=[jax.ShapeDtypeStruct((n, MEM_HEADS, MEM_HD), F32)] * 2,
        compiler_params=_params("parallel"),
        name="mem_kv",
    )(mem, wk_bf, wv_bf)


def _memattn_kernel(x_ref, mk_ref, mv_ref, wq_ref, wo_ref, g_ref, b_ref, y_ref, o_buf, mk_buf, mv_buf):
    nb, tm, _ = x_ref.shape
    @pl.when(pl.program_id(1) == 0)
    def _():
        for i in range(nb):
            for h in range(MEM_HEADS):
                mk_buf[i, h] = mk_ref[i, :, h, :].astype(BF16)
                mv_buf[i, h] = mv_ref[i, :, h, :].astype(BF16)

    x = x_ref[...].reshape(nb * tm, D_MODEL)
    q = jnp.dot(x.astype(BF16), wq_ref[...], preferred_element_type=F32) * (MEM_HD ** -0.5)
    for i in range(nb):
        rs = slice(i * tm, (i + 1) * tm)
        for h in range(MEM_HEADS):
            cs = slice(h * MEM_HD, (h + 1) * MEM_HD)
            s = _dot_nt(q[rs, cs], mk_buf[i, h])
            e = jnp.exp(s - jnp.max(s, -1, keepdims=True))
            o_buf[rs, cs] = _dot(e, mv_buf[i, h]) / jnp.sum(e, -1, keepdims=True)
    att = jnp.dot(o_buf[...].astype(BF16), wo_ref[...], preferred_element_type=F32)
    y = _layer_norm(DN_ALPHA * x + att, g_ref[...], b_ref[...])
    y_ref[...] = y.reshape(nb, tm, D_MODEL)


def _memattn(x, mem_k, mem_v, wq_bf, wo_bf, g, b, *, nb, tm):
    bsz, t, _ = x.shape
    n_mem = mem_k.shape[1]
    const = lambda a: pl.BlockSpec(a.shape, lambda i, j: (0, 0))
    return pl.pallas_call(
        _memattn_kernel,
        grid=(bsz // nb, t // tm),
        in_specs=[pl.BlockSpec((nb, tm, D_MODEL), lambda i, j: (i, j, 0)),
                  pl.BlockSpec((nb, n_mem, MEM_HEADS, MEM_HD), lambda i, j: (i, 0, 0, 0)),
                  pl.BlockSpec((nb, n_mem, MEM_HEADS, MEM_HD), lambda i, j: (i, 0, 0, 0)),
                  const(wq_bf), const(wo_bf), const(g), const(b)],
        out_specs=pl.BlockSpec((nb, tm, D_MODEL), lambda i, j: (i, j, 0)),
        out_shape=jax.ShapeDtypeStruct((bsz, t, D_MODEL), F32),
        scratch_shapes=[pltpu.VMEM((nb * tm, D_MODEL), F32),
                        pltpu.VMEM((nb, MEM_HEADS, n_mem, MEM_HD), BF16),
                        pltpu.VMEM((nb, MEM_HEADS, n_mem, MEM_HD), BF16)],
        compiler_params=_params("parallel", "arbitrary"),
        name="mem_attn_ln2",
    )(x, mem_k, mem_v, wq_bf, wo_bf, g, b)


def _take_top(s, idx, n_take, tie_safe):
    rank = jnp.full(s.shape, float(n_take), F32)
    tops = []
    for i in range(n_take):
        m = jnp.max(s, axis=0, keepdims=True)
        hit = s == m
        if tie_safe:
            first = jnp.min(jnp.where(hit, idx, 1e9), axis=0, keepdims=True)
            hit = idx == first
        rank = jnp.where(hit, float(i), rank)
        s = jnp.where(hit, -jnp.inf, s)
        tops.append(m)
    return rank, tops, s


def _n_removed(s):
    return jnp.sum(jnp.where(s == -jnp.inf, 1.0, 0.0), axis=0, keepdims=True)


def _peer_route(h, lanes, qh_ref, sk_ref, lr_ref, e1_ref, rank2_ref, e2_ref, tie_safe):
    k = PEER_TOPK
    s1 = _dot_nt(sk_ref[h, 0], qh_ref[h, lanes, 0:PEER_KEY_DIM])
    s2 = _dot_nt(sk_ref[h, 1], qh_ref[h, lanes, PEER_KEY_DIM:2 * PEER_KEY_DIM])
    kidx = lax.broadcasted_iota(jnp.int32, (PEER_NKEYS, 1), 0).astype(F32)
    rank1, top1, rest1 = _take_top(s1, kidx, k, tie_safe)
    rank2, top2, rest2 = _take_top(s2, kidx, k, tie_safe)
    t2 = jnp.concatenate(top2, axis=0)
    r8 = lax.broadcasted_iota(jnp.int32, (SUBLANES, 1), 0).astype(F32)
    r16 = lax.broadcasted_iota(jnp.int32, (2 * SUBLANES, 1), 0).astype(F32)
    blocks = [top1[0] + t2]
    flat = [r16]
    off = [jnp.zeros_like(r16)]
    for i in range(1, SUBLANES):
        blocks.append(top1[i] + t2[:SUBLANES])
        flat.append(float(i * k) + r8)
        off.append(jnp.where(r8 < float(k // (i + 1)), 0.0, -jnp.inf))
    blocks.append(jnp.concatenate(top1[SUBLANES:], axis=0) + top2[0])
    flat.append(float(k) * (r8 + float(SUBLANES)))
    off.append(jnp.zeros_like(r8))
    cand = jnp.concatenate(blocks, axis=0) + jnp.concatenate(off, axis=0)
    _, ctops, rest_c = _take_top(cand, jnp.concatenate(flat, axis=0), k, tie_safe)
    cnt = jnp.where(rest_c == -jnp.inf, 1.0, 0.0) - jnp.where(cand == -jnp.inf, 1.0, 0.0)
    z = jnp.sum(cnt * jnp.exp(cand - ctops[0]), axis=0, keepdims=True)
    lrow = jnp.zeros(s1.shape, F32)
    row0 = 0
    for i in range(SUBLANES):
        n_rows = 2 * SUBLANES if i == 0 else SUBLANES
        l_i = jnp.sum(cnt[row0:row0 + n_rows], axis=0, keepdims=True)
        lrow = jnp.where(rank1 == float(i), l_i, lrow)
        row0 += n_rows
    for r in range(SUBLANES):
        lrow = jnp.where(rank1 == float(SUBLANES + r), cnt[row0 + r:row0 + r + 1], lrow)
    lane_tile = lanes.start // LANES
    lr_ref[h, lane_tile] = lrow
    e1_ref[h, lane_tile] = jnp.exp(s1 - top1[0]) * (1.0 / z)
    rank2_ref[h, :, lanes] = rank2.astype(BF16)
    e2_ref[h, :, lanes] = jnp.exp(s2 - top2[0]).astype(BF16)
    bad = (jnp.abs(_n_removed(rest1) - float(k)) + jnp.abs(_n_removed(rest2) - float(k))
           + jnp.abs(jnp.sum(cnt, axis=0, keepdims=True) - float(k)))
    return jnp.max(bad) > 0.5


def _peer_kernel(x_ref, wq_ref, sk_ref, u_ref, vt_ref, g_ref, b_ref, y_ref,
                 xt_ref, qh_ref, lr_ref, e1_ref, rank2_ref, e2_ref, acc_ref):
    kb = pl.program_id(1)
    tm = x_ref.shape[0]
    eb = u_ref.shape[0]
    a_per_block = eb // PEER_NKEYS

    @pl.when(kb == 0)
    def _():
        x = x_ref[...]
        xt_ref[...] = x.T.astype(BF16)
        acc_ref[...] = jnp.zeros_like(acc_ref)
        qh =jnp.dot(x.astype(BF16), wq_ref[...], preferred_element_type=F32)
        for h in range(PEER_HEADS):
            qh_ref[h] = qh[:, 2 * h * PEER_KEY_DIM:2 * (h + 1) * PEER_KEY_DIM].astype(BF16)
        refs = (qh_ref, sk_ref, lr_ref, e1_ref, rank2_ref, e2_ref)
        lane_tiles = [slice(lt * LANES, (lt + 1) * LANES) for lt in range(tm // LANES)]
        group = PEER_ROUTE_GROUP if len(lane_tiles) % PEER_ROUTE_GROUP == 0 else 1
        for g0 in range(0, len(lane_tiles), group):
            tiles = lane_tiles[g0:g0 + group]

            def route(h, carry, tiles=tiles):
                ties = [_peer_route(h, lanes, *refs, tie_safe=False) for lanes in tiles]
                tie = ties[0]
                for t in ties[1:]:
                    tie = jnp.logical_or(tie, t)

                @pl.when(tie)
                def _():
                    for lanes in tiles:
                        _peer_route(h, lanes, *refs, tie_safe=True)
                return carry

            lax.fori_loop(0, PEER_HEADS, route, 0)

    bcast_rows = lambda r: jnp.concatenate([r] * (PEER_NKEYS // BF16_ROWS), axis=0)
    ht = jnp.dot(u_ref[...], xt_ref[...], preferred_element_type=F32)
    gh = _gelu(ht.astype(BF16))
    wh = []
    for al in range(a_per_block):
        a = kb * a_per_block + al
        tiles = []
        for lt in range(tm // LANES):
            lanes = slice(lt * LANES, (lt + 1) * LANES)
            w = jnp.zeros((PEER_NKEYS, LANES), BF16)
            for h in range(PEER_HEADS):
                lr = bcast_rows(jnp.broadcast_to(lr_ref[h, lt, pl.ds(a, 1), :], (BF16_ROWS, LANES)).astype(BF16))
                e1 = bcast_rows(jnp.broadcast_to(e1_ref[h, lt, pl.ds(a, 1), :], (BF16_ROWS, LANES)).astype(BF16))
                w = w + jnp.where(rank2_ref[h, :, lanes] < lr, e2_ref[h, :, lanes], jnp.zeros_like(w)) * e1
            tiles.append(w)
        wh.append(jnp.concatenate(tiles, axis=1) * gh[al * PEER_NKEYS:(al + 1) * PEER_NKEYS])
    acc_ref[...] += jnp.dot(vt_ref[...], jnp.concatenate(wh, axis=0), preferred_element_type=F32)

    @pl.when(kb == pl.num_programs(1) - 1)
    def _():
        y_ref[...] = _layer_norm(DN_ALPHA * x_ref[...] + acc_ref[...].T, g_ref[...], b_ref[...])


def _peer(x, wq_bf, sk_bf, u_bf, vt_bf, g, b):
    n_tokens = x.shape[0]
    n = -(-n_tokens // LANES) * LANES
    x = jnp.pad(x, ((0, n - n_tokens), (0, 0)))
    tm = ROW_TILE if n % ROW_TILE == 0 else LANES
    n_exp = u_bf.shape[0]
    eb = PEER_EXPERT_BLOCK
    once = pl.Buffered(1)
    const2 = lambda a: pl.BlockSpec(a.shape, lambda i, k: (0, 0), pipeline_mode=once)
    return pl.pallas_call(
        _peer_kernel,
        grid=(n // tm, n_exp // eb),
        in_specs=[pl.BlockSpec((tm, D_MODEL), lambda i, k: (i, 0)),
                  const2(wq_bf),
                  pl.BlockSpec(sk_bf.shape, lambda i, k: (0, 0, 0, 0), pipeline_mode=once),
                  pl.BlockSpec((eb, D_MODEL), lambda i, k: (k, 0)),
                  pl.BlockSpec((D_MODEL, eb), lambda i, k: (0, k)),
                  const2(g), const2(b)],
        out_specs=pl.BlockSpec((tm, D_MODEL), lambda i, k: (i, 0)),
        out_shape=jax.ShapeDtypeStruct((n, D_MODEL), F32),
        scratch_shapes=[pltpu.VMEM((D_MODEL, tm), BF16),
                        pltpu.VMEM((PEER_HEADS, tm, 2 * PEER_KEY_DIM), BF16),
                        pltpu.VMEM((PEER_HEADS, tm // LANES, PEER_NKEYS, LANES), F32),
                        pltpu.VMEM((PEER_HEADS, tm // LANES, PEER_NKEYS, LANES), F32),
                        pltpu.VMEM((PEER_HEADS, PEER_NKEYS, tm), BF16),
                        pltpu.VMEM((PEER_HEADS, PEER_NKEYS, tm), BF16),
                        pltpu.VMEM((D_MODEL, tm), F32)],
        compiler_params=_params("parallel", "arbitrary"),
        name="peer_ln3",
    )(x, wq_bf, sk_bf, u_bf, vt_bf, g, b)[:n_tokens]


def kernel(x_prompt, x_sample, cache_k_cmp, cache_v_cmp, cache_k_slc, cache_v_slc, cache_k_win, cache_v_win,
           state_hgrn, cache_mem_k, cache_mem_v, page_table, mem_prompt, w_in, hgrn_lb_logits, hgrn_norm_g,
           cmp_k_w1, cmp_k_b1, cmp_k_w2, cmp_k_b2, cmp_v_w1, cmp_v_b1, cmp_v_w2, cmp_v_b2, w_out, ln1_g, ln1_b,
           w_mem_q, w_mem_k, w_mem_v, w_mem_o, ln2_g, ln2_b, peer_w_q, peer_sub_keys, peer_u, peer_v,
           ln3_g, ln3_b):
    bsz, seq, _ = x_prompt.shape
    db, dt, _ = x_sample.shape
    dtp = -(-dt // SUBLANES) * SUBLANES
    layer = 0

    w_in_pad = jnp.pad(w_in[layer], ((0, 0), (0, sum(IN_GROUP_WIDTHS) - IN_COLS))).astype(BF16)
    kv0 = 4 * HG_WIDTH + NSA_WIDTH
    w_in_main = jnp.concatenate([w_in_pad[:, :kv0 + 2 * KV_WIDTH], w_in_pad[:, kv0 + 6 * KV_WIDTH:]], axis=1)
    w_in_kv_t = w_in_pad[:, kv0:kv0 + 6 * KV_WIDTH].T
    kwts = _cmp_weights(cmp_k_w1[layer], cmp_k_b1[layer], cmp_k_w2[layer], cmp_k_b2[layer])
    vwts = _cmp_weights(cmp_v_w1[layer], cmp_v_b1[layer], cmp_v_w2[layer], cmp_v_b2[layer])
    w_out_bf = w_out[layer].astype(BF16)
    wmq, wmk, wmv, wmo = (w[layer].astype(BF16) for w in (w_mem_q, w_mem_k, w_mem_v, w_mem_o))
    pwq = peer_w_q[layer].astype(BF16)
    psk = peer_sub_keys[layer].astype(BF16)
    pu = peer_u[layer].astype(BF16)
    pvt = peer_v[layer].astype(BF16).T
    vec = lambda a: a[layer].reshape(1, D_MODEL)
    g1, b1, g2, b2, g3, b3 = (vec(a) for a in (ln1_g, ln1_b, ln2_g, ln2_b, ln3_g, ln3_b))
    norm_g = hgrn_norm_g[layer]

    n_p = bsz * seq
    xp = x_prompt.reshape(n_p, D_MODEL)
    zh, nq, kc, vc, ngate, kc_t, vc_t, ks_t, vs_t, kw_t, vw_t = _inproj_prompt(xp, w_in_main, w_in_kv_t, bsz, seq)
    per_b = lambda a: a.reshape(bsz, seq, a.shape[-1])
    chunk = min(HG_CHUNK, seq)
    o_h, s_p = _hgrn(per_b(zh), hgrn_lb_logits, norm_g, jnp.zeros((bsz, HG_HEADS, HG_D, HG_D), F32),
                     chunk=chunk, step_tokens=min(HG_STEP_TOKENS, seq), t_valid=chunk)
    kblk, vblk = _cmp_prompt(per_b(kc), per_b(vc), kwts, vwts)
    o_n = _nsa_prompt(per_b(nq), per_b(ngate), kblk, vblk, ks_t, vs_t, kw_t, vw_t)
    x1 = _outproj(o_h.reshape(n_p, HG_WIDTH), o_n.reshape(n_p, NSA_WIDTH), xp, w_out_bf, g1, b1)
    n_mem = mem_prompt.shape[1]
    mem_k, mem_v = _memkv(mem_prompt.reshape(bsz * n_mem, D_MODEL), wmk, wmv)
    mem_k = mem_k.reshape(bsz, n_mem, MEM_HEADS, MEM_HD)
    mem_v = mem_v.reshape(bsz, n_mem, MEM_HEADS, MEM_HD)
    x2 = _memattn(x1.reshape(bsz, seq, D_MODEL), mem_k, mem_v, wmq, wmo, g2, b2, nb=1, tm=min(ROW_TILE, seq))
    y_p = _peer(x2.reshape(n_p, D_MODEL), pwq, psk, pu, pvt, g3, b3).reshape(bsz, seq, D_MODEL)

    kv5 = lambda a: jnp.transpose(a.reshape(bsz, NSA_GROUPS, NSA_HD, a.shape[-1]), (0, 3, 1, 2))[None]
    n_win = min(WINDOW, seq)
    win5 = lambda a: kv5(a[:, :, seq - n_win:])
    mem5 = lambda a: a[None]

    n_s = db * dtp
    xs = jnp.pad(x_sample, ((0, 0), (0, dtp - dt), (0, 0))).reshape(n_s, D_MODEL)
    zh, nq, kc_s, vc_s, ks_s, vs_s, kw_s, vw_s, ngate = _inproj(xs, w_in_pad)
    per_s = lambda a: a.reshape(db, dtp, a.shape[-1])
    o_h, s_s = _hgrn(per_s(zh), hgrn_lb_logits, norm_g, state_hgrn[layer],
                     chunk=dtp, step_tokens=dtp, t_valid=dt)
    o_n = _nsa_sample(page_table, per_s(nq), per_s(ngate), per_s(ks_s), per_s(vs_s), per_s(kw_s), per_s(vw_s),
                      jnp.transpose(cache_k_win[layer], (0, 2, 3, 1)), jnp.transpose(cache_v_win[layer], (0, 2, 3, 1)),
                      kwts, vwts, cache_k_cmp[layer], cache_v_cmp[layer], cache_k_slc[layer], cache_v_slc[layer],
                      t_valid=dt)
    x1 = _outproj(o_h.reshape(n_s, HG_WIDTH), o_n.reshape(n_s, NSA_WIDTH), xs, w_out_bf, g1, b1)
    nb = min(4, db)
    x2 = _memattn(x1.reshape(db, dtp, D_MODEL), cache_mem_k[layer], cache_mem_v[layer],
                  wmq, wmo, g2, b2, nb=nb, tm=dtp)
    y_s = _peer(x2.reshape(n_s, D_MODEL), pwq, psk, pu, pvt, g3, b3).reshape(db, dtp, D_MODEL)[:, :dt]
    skv5 = lambda a: a.reshape(db, dtp, NSA_GROUPS, NSA_HD)[None, :, :dt]

    return (y_p, y_s, kv5(kc_t), kv5(vc_t), kv5(ks_t), kv5(vs_t), win5(kw_t), win5(vw_t), s_p[None],
            mem5(mem_k), mem5(mem_v),
            skv5(kc_s), skv5(vc_s), skv5(ks_s), skv5(vs_s), skv5(kw_s), skv5(vw_s), s_s[None])
```
